```python
import math
import jax
import jax.numpy as jnp
from jax import lax
import numpy as np

D_MODEL = 1024
BATCH = 4
SEQ = 8192
DEPTH = 4

GRID_W = 64
CTX_LEN = 256
Q_BLOCK = 128
ROPE_BASE = 10000.0
LN_EPS = 1e-5
RMS_EPS = 1e-6
DEEPNORM_ALPHA = (2.0 * DEPTH) ** 0.25
DEEPNORM_BETA = (8.0 * DEPTH) ** -0.25

DA_HEADS = 4
DA_QK = 64
DA_V = 128
DA_WIDTH = DA_HEADS * DA_V
DA_SCALE = DA_QK ** -0.5
MLA_HEADS = 8
MLA_Q_RANK = 256
MLA_KV_RANK = 128
MLA_NOPE = 64
MLA_ROPE = 32
MLA_V = 64
MLA_WIDTH = MLA_HEADS * MLA_V
MLA_SCALE = (MLA_NOPE + MLA_ROPE) ** -0.5
ML_HEADS = 4
ML_DH = 128
ML_WIDTH = ML_HEADS * ML_DH
ML_CONV = 5
ML_CHUNK = 128
NA_HEADS = 8
NA_DH = 64
NA_WIDTH = NA_HEADS * NA_DH
NA_ROWS = 8
NA_COLS = 16
NA_SCALE = NA_DH ** -0.5

EVEN_WIDTH = DA_WIDTH + MLA_WIDTH
ODD_WIDTH = ML_WIDTH + NA_WIDTH
EVEN_SPLIT = [DA_HEADS * DA_QK] * 4 + [DA_WIDTH, MLA_Q_RANK, MLA_KV_RANK, MLA_ROPE, EVEN_WIDTH]
ODD_SPLIT = [2 * ML_WIDTH, ML_WIDTH, ML_WIDTH, 4 * ML_HEADS, NA_WIDTH, NA_WIDTH, NA_WIDTH, ODD_WIDTH]
EVEN_IN = sum(EVEN_SPLIT)
ODD_IN = sum(ODD_SPLIT)
N_EVEN = (DEPTH + 1) // 2
N_ODD = DEPTH // 2
F32 = jnp.float32

kernel_name = 'hybrid_diffattn_mla_mlstm_natten_prefix'


def split_cols(z, sizes):
    return jnp.split(z, np.cumsum(sizes)[:-1].tolist(), axis=-1)


def to_heads(t, n):
    b, s, e = t.shape
    return t.reshape(b, s, n, e // n).transpose(0, 2, 1, 3)


def from_heads(t):
    b, n, s, d = t.shape
    return t.transpose(0, 2, 1, 3).reshape(b, s, n * d)


def layer_norm(x, g, b):
    xf = x.astype(F32)
    mu = jnp.mean(xf, -1, keepdims=True)
    var = jnp.mean(jnp.square(xf - mu), -1, keepdims=True)
    return ((xf - mu) * lax.rsqrt(var + LN_EPS) * g.astype(F32) + b.astype(F32)).astype(x.dtype)


def head_layer_norm(x):
    xf = x.astype(F32)
    mu = jnp.mean(xf, -1, keepdims=True)
    var = jnp.mean(jnp.square(xf - mu), -1, keepdims=True)
    return ((xf - mu) * lax.rsqrt(var + LN_EPS)).astype(x.dtype)


def rms_norm(x, g):
    xf = x.astype(F32)
    return (xf * lax.rsqrt(jnp.mean(xf * xf, -1, keepdims=True) + RMS_EPS) * g.astype(F32)).astype(x.dtype)


def adaln(cond, w, b):
    m = jnp.einsum('...d,de->...e', jax.nn.silu(cond), w) + b
    return jnp.split(m, 3, axis=-1)


def axial_rope(n_tokens, dim):
    t = jnp.arange(n_tokens)
    row = (t // GRID_W).astype(F32)
    col = (t % GRID_W).astype(F32)
    n_freq = dim // 4
    freqs = ROPE_BASE ** (-jnp.arange(n_freq, dtype=F32) / n_freq)
    ang = jnp.concatenate([row[:, None] * freqs, col[:, None] * freqs], axis=-1)
    return jnp.cos(ang), jnp.sin(ang)


def apply_rope(x, cos, sin):
    half = x.shape[-1] // 2
    xf = x.astype(F32)
    x1, x2 = xf[..., :half], xf[..., half:]
    return jnp.concatenate([x1 * cos - x2 * sin, x2 * cos + x1 * sin], axis=-1).astype(x.dtype)


def _to_blocks(q):
    b, h, s, d = q.shape
    return q.reshape(b, h, s // Q_BLOCK, Q_BLOCK, d).transpose(2, 0, 1, 3, 4)


def _from_blocks(o):
    nb, b, h, qb, d = o.shape
    return o.transpose(1, 2, 0, 3, 4).reshape(b, h, nb * qb, d)


def blocked_queries(fn, *qs):
    out = lax.map(lambda blk: fn(*blk), tuple(_to_blocks(q) for q in qs))
    return _from_blocks(out)


def softmax_attention(q, k, v, scale):
    def one(qb):
        s = jnp.einsum('bhqd,bhkd->bhqk', qb, k).astype(F32) * scale
        p = jax.nn.softmax(s, axis=-1).astype(v.dtype)
        return jnp.einsum('bhqk,bhkd->bhqd', p, v)
    return blocked_queries(one, q)


def diff_attention(q1, q2, k1, k2, v, lam):
    def one(a, b_):
        s1 = jnp.einsum('bhqd,bhkd->bhqk', a, k1).astype(F32) * DA_SCALE
        s2 = jnp.einsum('bhqd,bhkd->bhqk', b_, k2).astype(F32) * DA_SCALE
        p = jax.nn.softmax(s1, axis=-1) - lam * jax.nn.softmax(s2, axis=-1)
        return jnp.einsum('bhqk,bhkd->bhqd', p.astype(v.dtype), v)
    return blocked_queries(one, q1, q2)


def even_project(h, w_in, b_in, q_norm_g, kv_norm_g, w_uq, w_ukv, rope_da, rope_mla):
    z = jnp.einsum('bsd,de->bse', h, w_in) + b_in
    q1, q2, k1, k2, v, cq, ckv, k_rope, gate = split_cols(z, EVEN_SPLIT)
    q1, q2, k1, k2 = (to_heads(t, DA_HEADS) for t in (q1, q2, k1, k2))
    v = to_heads(v, DA_HEADS)
    qm = to_heads(jnp.einsum('bsr,re->bse', rms_norm(cq, q_norm_g), w_uq), MLA_HEADS)
    kvm = to_heads(jnp.einsum('bsr,re->bse', rms_norm(ckv, kv_norm_g), w_ukv), MLA_HEADS)
    q_nope, q_rope = qm[..., :MLA_NOPE], qm[..., MLA_NOPE:]
    k_nope, vm = kvm[..., :MLA_NOPE], kvm[..., MLA_NOPE:]
    k_rope = k_rope[:, None]
    if rope_da is not None:
        q1, q2, k1, k2 = (apply_rope(t, *rope_da) for t in (q1, q2, k1, k2))
        q_rope = apply_rope(q_rope, *rope_mla)
        k_rope = apply_rope(k_rope, *rope_mla)
    qm = jnp.concatenate([q_nope, q_rope], axis=-1)
    km = jnp.concatenate([k_nope, jnp.broadcast_to(k_rope, k_nope.shape[:-1] + (MLA_ROPE,))], axis=-1)
    return (q1, q2, qm, gate), (k1, k2, v, km, vm)


def even_mix(queries, keys, lam, lam_init, subln_g, w_out):
    q1, q2, qm, gate = queries
    k1, k2, vd, km, vm = keys
    oa = rms_norm(diff_attention(q1, q2, k1, k2, vd, lam), subln_g) * (1.0 - lam_init)
    ob = softmax_attention(qm, km, vm, MLA_SCALE)
    y = jnp.concatenate([from_heads(oa), from_heads(ob)], axis=-1) * jax.nn.silu(gate)
    return jnp.einsum('bse,ed->bsd', y, w_out)


def centred_depthwise_conv(x, w, b):
    y = lax.conv_general_dilated(x, w[:, None, :].astype(x.dtype), window_strides=(1,),
                                 padding=[(ML_CONV // 2, ML_CONV // 2)],
                                 dimension_numbers=('NWC', 'WIO', 'NWC'),
                                 feature_group_count=x.shape[-1])
    return y + b


def odd_project(h, w_in, b_in, conv_w, conv_b, f_bias):
    b, s, _ = h.shape
    z = jnp.einsum('bsd,de->bse', h, w_in) + b_in
    qk, v, o, gates, qn, kn, vn, gate = split_cols(z, ODD_SPLIT)
    qk = jax.nn.silu(centred_depthwise_conv(qk, conv_w, conv_b))
    q = to_heads(qk[..., :ML_WIDTH], ML_HEADS)
    k = to_heads(qk[..., ML_WIDTH:], ML_HEADS) * (ML_DH ** -0.5)
    v = to_heads(v, ML_HEADS)
    g = gates.astype(F32).reshape(b, s, 4, ML_HEADS).transpose(2, 0, 3, 1)
    fb = f_bias.astype(F32)
    fwd = (g[0], jax.nn.log_sigmoid(g[1] + fb[0][:, None]))
    bwd = (g[2], jax.nn.log_sigmoid(g[3] + fb[1][:, None]))
    na = tuple(t.reshape(b, s, NA_HEADS, NA_DH) for t in (qn, kn, vn))
    return (q, k, v, jax.nn.sigmoid(o), fwd, bwd), na, gate


def zero_state(b):
    return (jnp.zeros((b, ML_HEADS, ML_DH, ML_DH), F32), jnp.zeros((b, ML_HEADS, ML_DH), F32),
            jnp.zeros((b, ML_HEADS), F32))


def mlstm_chunkwise(q, k, v, log_i, log_f, state, with_output=True):
    b, nh, s, d = q.shape
    nc = s // ML_CHUNK

    def chunks(t):
        return jnp.moveaxis(t.reshape(t.shape[:2] + (nc, ML_CHUNK) + t.shape[3:]), 2, 0)

    lower = jnp.tril(jnp.ones((ML_CHUNK, ML_CHUNK), dtype=bool))

    def step(carry, inp):
        c_mat, n_vec, m_sc = carry
        qb, kb, vb, li, lf = inp
        qb, kb, vb = qb.astype(F32), kb.astype(F32), vb.astype(F32)
        bcum = jnp.cumsum(lf, axis=-1)
        btot = bcum[..., -1]
        g = btot[..., None] - bcum + li
        m_new = jnp.maximum(btot + m_sc, jnp.max(g, axis=-1))
        w_s = jnp.exp(g - m_new[..., None])
        decay = jnp.exp(btot + m_sc - m_new)
        c_new = decay[..., None, None] * c_mat + jnp.einsum('bhs,bhsd,bhse->bhde', w_s, kb, vb)
        n_new = decay[..., None] * n_vec + jnp.einsum('bhs,bhsd->bhd', w_s, kb)
        if not with_output:
            return (c_new, n_new, m_new), None
        dmat = jnp.where(lower, bcum[..., :, None] - bcum[..., None, :] + li[..., None, :], -jnp.inf)
        inter = bcum + m_sc[..., None]
        m_t = jnp.maximum(inter, jnp.max(dmat, axis=-1))
        w_ts = jnp.exp(dmat - m_t[..., None]) * jnp.einsum('bhtd,bhsd->bhts', qb, kb)
        w_c = jnp.exp(inter - m_t)
        num = jnp.einsum('bhts,bhsd->bhtd', w_ts, vb) + w_c[..., None] * jnp.einsum('bhtd,bhde->bhte', qb, c_mat)
        den = jnp.sum(w_ts, axis=-1) + w_c * jnp.einsum('bhtd,bhd->bht', qb, n_vec)
        h_t = num / jnp.maximum(jnp.abs(den), jnp.exp(-m_t))[..., None]
        return (c_new, n_new, m_new), h_t.astype(v.dtype)

    state, hs = lax.scan(step, state, tuple(chunks(t) for t in (q, k, v, log_i, log_f)))
    if not with_output:
        return None, state
    return jnp.moveaxis(hs, 0, 2).reshape(b, nh, s, d), state


def flip_seq(t):
    return jnp.flip(t, axis=2)


def mlstm_bidirectional(ml, state_f, state_b, with_output=True):
    q, k, v, _, (li_f, lf_f), (li_b, lf_b) = ml
    h_f, st_f = mlstm_chunkwise(q, k, v, li_f, lf_f, state_f, with_output)
    h_b, st_b = mlstm_chunkwise(flip_seq(q), flip_seq(k), flip_seq(v), flip_seq(li_b), flip_seq(lf_b),
                                state_b, with_output)
    h = h_f + flip_seq(h_b) if with_output else None
    return h, st_f, st_b


def neighbourhood_attention(q, k, v, k_ctx, v_ctx, rpb, rows):
    b, s, h, d = q.shape
    wr = min(NA_ROWS, rows)
    n_nb = wr * NA_COLS
    qg, kg, vg = (t.reshape(b, rows, GRID_W, h, d) for t in (q, k, v))
    col = jnp.arange(GRID_W)
    col_start = jnp.clip(col - NA_COLS // 2, 0, GRID_W - NA_COLS)
    col_idx = col_start[:, None] + jnp.arange(NA_COLS)[None, :]
    col_off = col_idx - col[:, None] + (NA_COLS - 1)

    def one_row(r):
        rs = jnp.clip(r - wr // 2, 0, rows - wr)
        k_nb = lax.dynamic_slice_in_dim(kg, rs, wr, axis=1)[:, :, col_idx]
        v_nb = lax.dynamic_slice_in_dim(vg, rs, wr, axis=1)[:, :, col_idx]
        q_r = lax.dynamic_index_in_dim(qg, r, axis=1, keepdims=False)
        row_off = rs + jnp.arange(wr) - r + (NA_ROWS - 1)
        bias = rpb[:, row_off][:, :, col_off].transpose(0, 2, 1, 3).astype(F32)
        s_nb = jnp.einsum('bqhd,brqjhd->bhqrj', q_r, k_nb).astype(F32) * NA_SCALE + bias
        s_cx = jnp.einsum('bqhd,bthd->bhqt', q_r, k_ctx).astype(F32) * NA_SCALE
        p = jax.nn.softmax(jnp.concatenate([s_nb.reshape(b, h, GRID_W, n_nb), s_cx], axis=-1), axis=-1)
        p = p.astype(v.dtype)
        p_nb = p[..., :n_nb].reshape(b, h, GRID_W, wr, NA_COLS)
        return (jnp.einsum('bhqrj,brqjhd->bqhd', p_nb, v_nb)
                + jnp.einsum('bhqt,bthd->bqhd', p[..., n_nb:], v_ctx))

    o = lax.map(one_row, jnp.arange(rows))
    return o.transpose(1, 0, 2, 3, 4).reshape(b, s, h, d)


def odd_mix(h_ml, o_gate, na_out, gate, norm_g, w_out):
    b, s = na_out.shape[:2]
    y_ml = from_heads(head_layer_norm(h_ml)) * norm_g * o_gate
    y = jnp.concatenate([y_ml, na_out.reshape(b, s, NA_WIDTH)], axis=-1) * jax.nn.silu(gate)
    return jnp.einsum('bse,ed->bsd', y, w_out)


def setup_inputs(seed: int = 0) -> dict:
    key = jax.random.key(seed)
    ks = iter(jax.random.split(key, 32))

    def nrm(shape, scale):
        return jax.random.normal(next(ks), shape, F32) * scale

    d = D_MODEL
    return {
        'x': nrm((BATCH, SEQ, d), 1.0),
        'c': nrm((BATCH, d), 1.0),
        'ctx': nrm((BATCH, CTX_LEN, d), 1.0),
        'c_ctx': nrm((d,), 1.0),
        'ada_w': nrm((DEPTH, d, 3 * d), d ** -0.5),
        'ada_b': nrm((DEPTH, 3 * d), 0.02),
        'ln_g': 1.0 + nrm((DEPTH, d), 0.01),
        'ln_b': nrm((DEPTH, d), 0.01),
        'ev_w_in': nrm((N_EVEN, d, EVEN_IN), d ** -0.5),
        'ev_b_in': nrm((N_EVEN, EVEN_IN), 0.01),
        'da_lambda': nrm((N_EVEN, 4, DA_QK), 0.1),
        'da_subln_g': 1.0 + nrm((N_EVEN, DA_V), 0.01),
        'mla_q_norm_g': 1.0 + nrm((N_EVEN, MLA_Q_RANK), 0.01),
        'mla_kv_norm_g': 1.0 + nrm((N_EVEN, MLA_KV_RANK), 0.01),
        'mla_w_uq': nrm((N_EVEN, MLA_Q_RANK, MLA_HEADS * (MLA_NOPE + MLA_ROPE)), MLA_Q_RANK ** -0.5),
        'mla_w_ukv': nrm((N_EVEN, MLA_KV_RANK, MLA_HEADS * (MLA_NOPE + MLA_V)), MLA_KV_RANK ** -0.5),
        'ev_w_out': nrm((N_EVEN, EVEN_WIDTH, d), EVEN_WIDTH ** -0.5 * DEEPNORM_BETA),
        'od_w_in': nrm((N_ODD, d, ODD_IN), d ** -0.5),
        'od_b_in': nrm((N_ODD, ODD_IN), 0.01),
        'ml_conv_w': nrm((N_ODD, ML_CONV, 2 * ML_WIDTH), ML_CONV ** -0.5),
        'ml_conv_b': nrm((N_ODD, 2 * ML_WIDTH), 0.01),
        'ml_f_bias': jnp.linspace(3.0, 6.0, ML_HEADS, dtype=F32) + nrm((N_ODD, 2, ML_HEADS), 0.01),
        'ml_norm_g': 1.0 + nrm((N_ODD, ML_WIDTH), 0.01),
        'na_rpb': nrm((N_ODD, NA_HEADS, 2 * NA_ROWS - 1, 2 * NA_COLS - 1), 0.02),
        'od_w_out': nrm((N_ODD, ODD_WIDTH, d), ODD_WIDTH ** -0.5 * DEEPNORM_BETA),
    }


def reference(x, c, ctx, c_ctx, ada_w, ada_b, ln_g, ln_b, ev_w_in, ev_b_in, da_lambda, da_subln_g,
              mla_q_norm_g, mla_kv_norm_g, mla_w_uq, mla_w_ukv, ev_w_out, od_w_in, od_b_in,
              ml_conv_w, ml_conv_b, ml_f_bias, ml_norm_g, na_rpb, od_w_out):
    n_lat = x.shape[1]
    rows = n_lat // GRID_W
    rope_da = axial_rope(n_lat, DA_QK)
    rope_mla = axial_rope(n_lat, MLA_ROPE)
    xl, xc = x, ctx
    for l in range(DEPTH):
        update_ctx = l < DEPTH - 1
        sh_l, sc_l, g_l = adaln(c, ada_w[l], ada_b[l])
        sh_c, sc_c, g_c = adaln(c_ctx, ada_w[l], ada_b[l])
        hl = xl * (1.0 + sc_l[:, None]) + sh_l[:, None]
        hc = xc * (1.0 + sc_c) + sh_c
        i = l // 2
        yc = None
        if l % 2 == 0:
            lam_init = 0.8 - 0.6 * math.exp(-0.3 * l)
            lq1, lk1, lq2, lk2 = da_lambda[i].astype(F32)
            lam = jnp.exp(jnp.sum(lq1 * lk1)) - jnp.exp(jnp.sum(lq2 * lk2)) + lam_init
            ql, kl = even_project(hl, ev_w_in[i], ev_b_in[i], mla_q_norm_g[i], mla_kv_norm_g[i],
                                  mla_w_uq[i], mla_w_ukv[i], rope_da, rope_mla)
            qc, kc = even_project(hc, ev_w_in[i], ev_b_in[i], mla_q_norm_g[i], mla_kv_norm_g[i],
                                  mla_w_uq[i], mla_w_ukv[i], None, None)
            keys_l = tuple(jnp.concatenate([a, b_], axis=2) for a, b_ in zip(kl, kc))
            yl = even_mix(ql, keys_l, lam, lam_init, da_subln_g[i], ev_w_out[i])
            if update_ctx:
                yc = even_mix(qc, kc, lam, lam_init, da_subln_g[i], ev_w_out[i])
        else:
            ml_l, na_l, gate_l = odd_project(hl, od_w_in[i], od_b_in[i], ml_conv_w[i], ml_conv_b[i], ml_f_bias[i])
            ml_c, na_c, gate_c = odd_project(hc, od_w_in[i], od_b_in[i], ml_conv_w[i], ml_conv_b[i], ml_f_bias[i])
            z0 = zero_state(xc.shape[0])
            h_c, st_f, st_b = mlstm_bidirectional(ml_c, z0, z0, update_ctx)
            h_l, _, _ = mlstm_bidirectional(ml_l, st_f, st_b)
            na_out_l = neighbourhood_attention(na_l[0], na_l[1], na_l[2], na_c[1], na_c[2], na_rpb[i], rows)
            yl = odd_mix(h_l, ml_l[3], na_out_l, gate_l, ml_norm_g[i], od_w_out[i])
            if update_ctx:
                qn, kn, vn = (t.transpose(0, 2, 1, 3) for t in na_c)
                na_out_c = softmax_attention(qn, kn, vn, NA_SCALE).transpose(0, 2, 1, 3)
                yc = odd_mix(h_c, ml_c[3], na_out_c, gate_c, ml_norm_g[i], od_w_out[i])
        xl = layer_norm(DEEPNORM_ALPHA * xl + g_l[:, None] * yl, ln_g[l], ln_b[l])
        if update_ctx:
            xc = layer_norm(DEEPNORM_ALPHA * xc + g_c * yc, ln_g[l], ln_b[l])
    return xl
```

```python
import functools
import math

import numpy as np
import jax
import jax.numpy as jnp
from jax import lax
from jax.experimental import pallas as pl
from jax.experimental.pallas import tpu as pltpu

F32 = jnp.float32
BF16 = jnp.bfloat16

D_MODEL = 1024
DEPTH = 4
GRID_W = 64
ROPE_BASE = 10000.0
LN_EPS = 1e-5
RMS_EPS = 1e-6
DEEPNORM_ALPHA = (2.0 * DEPTH) ** 0.25

DA_HEADS, DA_QK, DA_V = 4, 64, 128
DA_SCALE = DA_QK ** -0.5
MLA_HEADS, MLA_Q_RANK, MLA_KV_RANK, MLA_NOPE, MLA_ROPE, MLA_V = 8, 256, 128, 64, 32, 64
MLA_SCALE = (MLA_NOPE + MLA_ROPE) ** -0.5
ML_HEADS, ML_DH, ML_CONV, ML_CHUNK = 4, 128, 5, 128
ML_WIDTH = ML_HEADS * ML_DH
NA_HEADS, NA_DH, NA_ROWS, NA_COLS = 8, 64, 8, 16
NA_WIDTH = NA_HEADS * NA_DH
NA_SCALE = NA_DH ** -0.5

LANES = 128
MASK_VALUE = -1e30
VMEM_LIMIT = 48 * 1024 * 1024

EV_COLS = 3200
EV_V, EV_CQ, EV_CKV, EV_KRA, EV_KRB, EV_GATE = 1024, 1536, 1792, 1920, 2048, 2176
OD_COLS = 4736
OD_V, OD_O, OD_GATES, OD_QN, OD_KN, OD_VN, OD_GATE = 1024, 1536, 2048, 2176, 2688, 3200, 3712
QT_ROWS = 1280
N_UNITS = 16


def _cparams(sem):
    return pltpu.CompilerParams(dimension_semantics=sem, vmem_limit_bytes=VMEM_LIMIT)


def _silu(x):
    return x * jax.nn.sigmoid(x)


def _dot(a, b):
    return jnp.dot(a, b, preferred_element_type=F32)


def _log_sigmoid(x):
    return jnp.minimum(x, 0.0) - jnp.log(1.0 + jnp.exp(-jnp.abs(x)))


def _ada_kernel(c_ref, w_ref, b_ref, o_ref):
    c = c_ref[...]
    o_ref[0] = _dot(_silu(c).astype(BF16), w_ref[0].astype(BF16)) + b_ref[0]


def _ada_mod(cond8, ada_w, ada_b):
    depth, d, e = ada_w.shape
    tn = 1024
    return pl.pallas_call(
        _ada_kernel,
        grid=(depth, e // tn),
        in_specs=[pl.BlockSpec((8, d), lambda l, n: (0, 0)),
                  pl.BlockSpec((1, d, tn), lambda l, n: (l, 0, n)),
                  pl.BlockSpec((1, 1, tn), lambda l, n: (l, 0, n))],
        out_specs=pl.BlockSpec((1, 8, tn), lambda l, n: (l, 0, n)),
        out_shape=jax.ShapeDtypeStruct((depth, 8, e), F32),
        compiler_params=_cparams(("parallel", "parallel")),
        name="ada_mod",
    )(cond8, ada_w, ada_b.reshape(depth, 1, e))


def _mod_spec(mod):
    per_batch = mod.shape[0] > 1
    return pl.BlockSpec((1, 1, mod.shape[2]), (lambda b, i: (b, 0, 0)) if per_batch else (lambda b, i: (0, 0, 0)))


def _proj_even_kernel(x_ref, mod_ref, rope_ref, w_ref, b_ref, qg_ref, kvg_ref, wuq_ref, wukv_ref,
                      qt_ref, ka_ref, km_ref, vt_ref, g_ref):
    d = D_MODEL
    x = x_ref[0]
    mod = mod_ref[0]
    h = (x * (1.0 + mod[:, d:2 * d]) + mod[:, 0:d]).astype(BF16)
    z = _dot(h, w_ref[...]) + b_ref[...]
    rope = rope_ref[...]
    cos_a, sin_a, cos_m, sin_m, sin_ms = (rope[:, LANES * i:LANES * (i + 1)] for i in range(5))

    def blk(o):
        return z[:, o:o + LANES]

    def rot(a, b, c, s):
        return a * c - b * s, b * c + a * s

    q1a, q1b = rot(blk(0), blk(128), cos_a, sin_a)
    q2a, q2b = rot(blk(256), blk(384), cos_a, sin_a)
    k1a, k1b = rot(blk(512), blk(640), cos_a, sin_a)
    k2a, k2b = rot(blk(768), blk(896), cos_a, sin_a)
    for i, t in enumerate((q1a, q1b, q2a, q2b)):
        qt_ref[0, LANES * i:LANES * (i + 1), :] = (t * DA_SCALE).T.astype(BF16)
    for i, t in enumerate((k1a, k1b, k2a, k2b)):
        ka_ref[0, i // 2, :, LANES * (i % 2):LANES * (i % 2 + 1)] = t.astype(BF16)
    for i in range(4):
        vt_ref[0, LANES * i:LANES * (i + 1), :] = blk(EV_V + LANES * i).T.astype(BF16)

    def rms(t, g):
        return (t * lax.rsqrt(jnp.mean(t * t, axis=-1, keepdims=True) + RMS_EPS) * g).astype(BF16)

    qm = _dot(rms(z[:, EV_CQ:EV_CQ + MLA_Q_RANK], qg_ref[...]), wuq_ref[...])
    qra, qrb = rot(qm[:, 512:640], qm[:, 640:768], cos_m, sin_m)
    for i in range(4):
        qt_ref[0, 512 + LANES * i:512 + LANES * (i + 1), :] = (qm[:, LANES * i:LANES * (i + 1)] * MLA_SCALE).T.astype(BF16)
    qt_ref[0, 1024:1152, :] = (qra * MLA_SCALE).T.astype(BF16)
    qt_ref[0, 1152:1280, :] = (qrb * MLA_SCALE).T.astype(BF16)

    kvm = _dot(rms(z[:, EV_CKV:EV_CKV + MLA_KV_RANK], kvg_ref[...]), wukv_ref[...])
    k_rope = (blk(EV_KRA) * cos_m + blk(EV_KRB) * sin_ms).astype(BF16)
    for g in range(4):
        km_ref[0, g, :, 0:LANES] = kvm[:, LANES * g:LANES * (g + 1)].astype(BF16)
        km_ref[0, g, :, LANES:2 * LANES] = k_rope
        vt_ref[0, 512 + LANES * g:512 + LANES * (g + 1), :] = kvm[:, 512 + LANES * g:512 + LANES * (g + 1)].T.astype(BF16)

    g_ref[0] = _silu(z[:, EV_GATE:EV_GATE + d]).astype(BF16)


def _proj_even(x, mod, rope, w, b, qg, kvg, wuq, wukv, tm):
    bsz, n, d = x.shape
    const = lambda shape: pl.BlockSpec(shape, lambda b_, i: (0,) * len(shape))
    return pl.pallas_call(
        _proj_even_kernel,
        grid=(bsz, n // tm),
        in_specs=[pl.BlockSpec((1, tm, d), lambda b_, i: (b_, i, 0)),
                  _mod_spec(mod),
                  pl.BlockSpec((tm, 5 * LANES), lambda b_, i: (i, 0)),
                  const((d, EV_COLS)), const((1, EV_COLS)), const((1, MLA_Q_RANK)), const((1, MLA_KV_RANK)),
                  const((MLA_Q_RANK, 768)), const((MLA_KV_RANK, 1024))],
        out_specs=[pl.BlockSpec((1, QT_ROWS, tm), lambda b_, i: (b_, 0, i)),
                   pl.BlockSpec((1, 2, tm, 256), lambda b_, i: (b_, 0, i, 0)),
                   pl.BlockSpec((1, 4, tm, 256), lambda b_, i: (b_, 0, i, 0)),
                   pl.BlockSpec((1, 1024, tm), lambda b_, i: (b_, 0, i)),
                   pl.BlockSpec((1, tm, d), lambda b_, i: (b_, i, 0))],
        out_shape=[jax.ShapeDtypeStruct((bsz, QT_ROWS, n), BF16),
                   jax.ShapeDtypeStruct((bsz, 2, n, 256), BF16),
                   jax.ShapeDtypeStruct((bsz, 4, n, 256), BF16),
                   jax.ShapeDtypeStruct((bsz, 1024, n), BF16),
                   jax.ShapeDtypeStruct((bsz, n, d), BF16)],
        compiler_params=_cparams(("parallel", "parallel")),
        name="proj_even",
    )(x, mod, rope, w, b, qg, kvg, wuq, wukv)


def _attn_even_kernel(*refs, tq, nkv, has_ctx, lam_init):
    if has_ctx:
        (lam_ref, subg_ref, qt_ref, ka_ref, km_ref, vt_ref, kac_ref, kmc_ref, vtc_ref,
         y_ref, qw, acc_a, acc_m, m_s, l_s) = refs
    else:
        (lam_ref, subg_ref, qt_ref, ka_ref, km_ref, vt_ref,
         y_ref, qw, acc_a, acc_m, m_s, l_s) = refs
    j = pl.program_id(2)

    def process(ka, km, vt):
        def softmax_step(u, k):
            s = _dot(k, qw[u])
            m_old = m_s[u]
            m_new = jnp.maximum(m_old, jnp.max(s, axis=0, keepdims=True))
            alpha = jnp.exp(m_old - m_new)
            p = jnp.exp(s - m_new)
            l_s[u] = alpha * l_s[u] + jnp.sum(p, axis=0, keepdims=True)
            m_s[u] = m_new
            return alpha, p.astype(BF16)

        def unit_a(u, carry):
            alpha, p = softmax_step(u, ka[0, u >> 2])
            v = vt[0, pl.ds(pl.multiple_of((u & (DA_HEADS - 1)) * DA_V, DA_V), DA_V), :]
            acc_a[u] = alpha * acc_a[u] + _dot(v, p)
            return carry

        def unit_m(hd, carry):
            u = 2 * DA_HEADS + hd
            alpha, p = softmax_step(u, km[0, hd >> 1])
            v = vt[0, pl.ds(pl.multiple_of(DA_HEADS * DA_V + hd * MLA_V, MLA_V), MLA_V), :]
            acc_m[hd] = alpha * acc_m[hd] + _dot(v, p)
            return carry

        lax.fori_loop(0, 2 * DA_HEADS, unit_a, 0)
        lax.fori_loop(0, MLA_HEADS, unit_m, 0)

    @pl.when(j == 0)
    def _init():
        row = lax.broadcasted_iota(jnp.int32, (256, tq), 0)
        head_of_row = (row >> 5) & (DA_HEADS - 1)
        for mp in range(2):
            q = qt_ref[0, 256 * mp:256 * (mp + 1), :]
            for hd in range(DA_HEADS):
                qw[mp * DA_HEADS + hd] = jnp.where(head_of_row == hd, q, jnp.zeros_like(q))
        for hd in range(MLA_HEADS):
            u = 2 * DA_HEADS + hd
            qw[u] = jnp.zeros((256, tq), BF16)
            qw[u, 64 * (hd % 2):64 * (hd % 2) + 64, :] = qt_ref[0, 512 + 64 * hd:512 + 64 * hd + 64, :]
            qw[u, 128:144, :] = qt_ref[0, 1024 + 16 * hd:1024 + 16 * hd + 16, :]
            qw[u, 144:160, :] = qt_ref[0, 1152 + 16 * hd:1152 + 16 * hd + 16, :]
        acc_a[...] = jnp.zeros(acc_a.shape, F32)
        acc_m[...] = jnp.zeros(acc_m.shape, F32)
        m_s[...] = jnp.full(m_s.shape, MASK_VALUE, F32)
        l_s[...] = jnp.zeros(l_s.shape, F32)
        if has_ctx:
            process(kac_ref, kmc_ref, vtc_ref)

    process(ka_ref, km_ref, vt_ref)

    @pl.when(j == nkv - 1)
    def _finish():
        lm = lam_ref[...]
        lam = (jnp.exp(jnp.sum(lm[0:1] * lm[1:2], axis=-1, keepdims=True))
               - jnp.exp(jnp.sum(lm[2:3] * lm[3:4], axis=-1, keepdims=True)) + lam_init)
        for hd in range(DA_HEADS):
            o = acc_a[hd] / l_s[hd] - lam * (acc_a[DA_HEADS + hd] / l_s[DA_HEADS + hd])
            o = o * lax.rsqrt(jnp.mean(o * o, axis=0, keepdims=True) + RMS_EPS) * subg_ref[...] * (1.0 - lam_init)
            y_ref[0, :, DA_V * hd:DA_V * (hd + 1)] = o.T.astype(BF16)
        for g in range(MLA_HEADS // 2):
            o = jnp.concatenate([acc_m[2 * g] / l_s[2 * DA_HEADS + 2 * g],
                                 acc_m[2 * g + 1] / l_s[2 * DA_HEADS + 2 * g + 1]], axis=0)
            y_ref[0, :, 512 + LANES * g:512 + LANES * (g + 1)] = o.T.astype(BF16)


def _attn_even(lam_p, subg, qt, ka, km, vt, ctx_kv, lam_init, tq, tk):
    bsz, _, n = qt.shape
    nk = ka.shape[2]
    nq, nkv = n // tq, nk // tk
    has_ctx = ctx_kv is not None
    in_specs = [pl.BlockSpec((4, DA_QK), lambda b, i, j: (0, 0)),
                pl.BlockSpec((DA_V, 1), lambda b, i, j: (0, 0)),
                pl.BlockSpec((1, QT_ROWS, tq), lambda b, i, j: (b, 0, i)),
                pl.BlockSpec((1, 2, tk, 256), lambda b, i, j: (b, 0, j, 0)),
                pl.BlockSpec((1, 4, tk, 256), lambda b, i, j: (b, 0, j, 0)),
                pl.BlockSpec((1, 1024, tk), lambda b, i, j: (b, 0, j))]
    args = [lam_p, subg, qt, ka, km, vt]
    if has_ctx:
        tc = ctx_kv[0].shape[2]
        in_specs += [pl.BlockSpec((1, 2, tc, 256), lambda b, i, j: (b, 0, 0, 0)),
                     pl.BlockSpec((1, 4, tc, 256), lambda b, i, j: (b, 0, 0, 0)),
                     pl.BlockSpec((1, 1024, tc), lambda b, i, j: (b, 0, 0))]
        args += list(ctx_kv)
    return pl.pallas_call(
        functools.partial(_attn_even_kernel, tq=tq, nkv=nkv, has_ctx=has_ctx, lam_init=lam_init),
        grid=(bsz, nq, nkv),
        in_specs=in_specs,
        out_specs=pl.BlockSpec((1, tq, D_MODEL), lambda b, i, j: (b, i, 0)),
        out_shape=jax.ShapeDtypeStruct((bsz, n, D_MODEL), BF16),
        scratch_shapes=[pltpu.VMEM((N_UNITS, 256, tq), BF16),
                        pltpu.VMEM((2 * DA_HEADS, DA_V, tq), F32),
                        pltpu.VMEM((MLA_HEADS, MLA_V, tq), F32),
                        pltpu.VMEM((N_UNITS, 1, tq), F32),
                        pltpu.VMEM((N_UNITS, 1, tq), F32)],
        compiler_params=_cparams(("parallel", "parallel", "arbitrary")),
        name="attn_even",
    )(*args)


def _layer_norm_rows(t, g, b):
    mu = jnp.mean(t, axis=-1, keepdims=True)
    tc = t - mu
    var = jnp.mean(tc * tc, axis=-1, keepdims=True)
    return tc * lax.rsqrt(var + LN_EPS) * g + b


def _out_even_kernel(y_ref, g_ref, x_ref, mod_ref, w_ref, lng_ref, lnb_ref, o_ref):
    d = D_MODEL
    y = (y_ref[0].astype(F32) * g_ref[0].astype(F32)).astype(BF16)
    out = _dot(y, w_ref[...])
    t = DEEPNORM_ALPHA * x_ref[0] + mod_ref[0][:, 2 * d:3 * d] * out
    o_ref[0] = _layer_norm_rows(t, lng_ref[...], lnb_ref[...])


def _out_odd_kernel(hf_ref, hb_ref, so_ref, na_ref, g_ref, x_ref, mod_ref, w_ref, ng_ref, lng_ref, lnb_ref, o_ref):
    d = D_MODEL
    hsum = hf_ref[0] + hb_ref[0]
    parts = []
    for hd in range(ML_HEADS):
        t = hsum[:, ML_DH * hd:ML_DH * (hd + 1)]
        mu = jnp.mean(t, axis=-1, keepdims=True)
        tc = t - mu
        var = jnp.mean(tc * tc, axis=-1, keepdims=True)
        parts.append(tc * lax.rsqrt(var + LN_EPS))
    y_ml = jnp.concatenate(parts, axis=-1) * ng_ref[...] * so_ref[0].astype(F32)
    g = g_ref[0].astype(F32)
    y = jnp.concatenate([y_ml * g[:, 0:ML_WIDTH], na_ref[0].astype(F32) * g[:, ML_WIDTH:]], axis=-1).astype(BF16)
    out = _dot(y, w_ref[...])
    t = DEEPNORM_ALPHA * x_ref[0] + mod_ref[0][:, 2 * d:3 * d] * out
    o_ref[0] = _layer_norm_rows(t, lng_ref[...], lnb_ref[...])


def _out_proj(kernel_fn, name, acts, x, mod, w, extra, ln_g, ln_b, tm):
    bsz, n, d = x.shape
    tok = lambda a: pl.BlockSpec((1, tm, a.shape[2]), lambda b_, i: (b_, i, 0))
    const = lambda a: pl.BlockSpec(a.shape, lambda b_, i: (0,) * a.ndim)
    ins = list(acts) + [x, mod, w] + list(extra) + [ln_g, ln_b]
    specs = [tok(a) for a in acts] + [tok(x), _mod_spec(mod), const(w)] + [const(e) for e in extra] + [const(ln_g), const(ln_b)]
    return pl.pallas_call(
        kernel_fn,
        grid=(bsz, n // tm),
        in_specs=specs,
        out_specs=pl.BlockSpec((1, tm, d), lambda b_, i: (b_, i, 0)),
        out_shape=jax.ShapeDtypeStruct((bsz, n, d), F32),
        compiler_params=_cparams(("parallel", "parallel")),
        name=name,
    )(*ins)


def _proj_odd_kernel(x_ref, mod_ref, w_ref, b_ref, qk_ref, v_ref, so_ref, gt_ref, qn_ref, kn_ref, vn_ref, g_ref):
    d = D_MODEL
    x = x_ref[0]
    mod = mod_ref[0]
    h = (x * (1.0 + mod[:, d:2 * d]) + mod[:, 0:d]).astype(BF16)
    z = _dot(h, w_ref[...]) + b_ref[...]
    qk_ref[0] = z[:, 0:2 * ML_WIDTH]
    v_ref[0] = z[:, OD_V:OD_V + ML_WIDTH].astype(BF16)
    so_ref[0] = jax.nn.sigmoid(z[:, OD_O:OD_O + ML_WIDTH]).astype(BF16)
    gt_ref[0] = z[:, OD_GATES:OD_GATES + LANES]
    qn_ref[0] = (z[:, OD_QN:OD_QN + NA_WIDTH] * NA_SCALE).astype(BF16)
    kn_ref[0] = z[:, OD_KN:OD_KN + NA_WIDTH].astype(BF16)
    vn_ref[0] = z[:, OD_VN:OD_VN + NA_WIDTH].astype(BF16)
    g_ref[0] = _silu(z[:, OD_GATE:OD_GATE + d]).astype(BF16)


def _proj_odd(x, mod, w, b, tm):
    bsz, n, d = x.shape
    const = lambda shape: pl.BlockSpec(shape, lambda b_, i: (0,) * len(shape))
    widths = [(2 * ML_WIDTH, F32), (ML_WIDTH, BF16), (ML_WIDTH, BF16), (LANES, F32),
              (NA_WIDTH, BF16), (NA_WIDTH, BF16), (NA_WIDTH, BF16), (d, BF16)]
    return pl.pallas_call(
        _proj_odd_kernel,
        grid=(bsz, n // tm),
        in_specs=[pl.BlockSpec((1, tm, d), lambda b_, i: (b_, i, 0)), _mod_spec(mod),
                  const((d, OD_COLS)), const((1, OD_COLS))],
        out_specs=[pl.BlockSpec((1, tm, wd), lambda b_, i: (b_, i, 0)) for wd, _ in widths],
        out_shape=[jax.ShapeDtypeStruct((bsz, n, wd), dt) for wd, dt in widths],
        compiler_params=_cparams(("parallel", "parallel")),
        name="proj_odd",
    )(x, mod, w, b)


def _conv_kernel(cur_ref, prev_ref, next_ref, w_ref, b_ref, o_ref, buf, *, tm, nt):
    i = pl.program_id(1)
    buf[0:8, :] = jnp.where(i > 0, prev_ref[0], 0.0)
    buf[8:8 + tm, :] = cur_ref[0]
    buf[8 + tm:16 + tm, :] = jnp.where(i < nt - 1, next_ref[0], 0.0)
    w = w_ref[...]
    acc = b_ref[...] + buf[6:6 + tm, :] * w[0:1]
    for tap in range(1, ML_CONV):
        acc = acc + buf[6 + tap:6 + tap + tm, :] * w[tap:tap + 1]
    y = _silu(acc)
    o_ref[0, :, 0:ML_WIDTH] = y[:, 0:ML_WIDTH].astype(BF16)
    o_ref[0, :, ML_WIDTH:] = (y[:, ML_WIDTH:] * (ML_DH ** -0.5)).astype(BF16)


def _conv_silu(qk, w, b, tm):
    bsz, n, c = qk.shape
    nt = n // tm
    r = tm // 8
    last8 = n // 8 - 1
    return pl.pallas_call(
        functools.partial(_conv_kernel, tm=tm, nt=nt),
        grid=(bsz, nt),
        in_specs=[pl.BlockSpec((1, tm, c), lambda b_, i: (b_, i, 0)),
                  pl.BlockSpec((1, 8, c), lambda b_, i: (b_, jnp.maximum(i * r - 1, 0), 0)),
                  pl.BlockSpec((1, 8, c), lambda b_, i: (b_, jnp.minimum((i + 1) * r, last8), 0)),
                  pl.BlockSpec((ML_CONV, c), lambda b_, i: (0, 0)),
                  pl.BlockSpec((1, c), lambda b_, i: (0, 0))],
        out_specs=pl.BlockSpec((1, tm, c), lambda b_, i: (b_, i, 0)),
        out_shape=jax.ShapeDtypeStruct((bsz, n, c), BF16),
        scratch_shapes=[pltpu.VMEM((tm + 16, c), F32)],
        compiler_params=_cparams(("parallel", "parallel")),
        name="conv_silu",
    )(qk, qk, qk, w, b)


def _cumsum_rows(x, reverse):
    n = x.shape[0]
    row = lax.broadcasted_iota(jnp.int32, x.shape, 0)
    sh = 1
    while sh < n:
        if reverse:
            x = x + jnp.where(row + sh < n, pltpu.roll(x, n - sh, 0), 0.0)
        else:
            x = x + jnp.where(row >= sh, pltpu.roll(x, sh, 0), 0.0)
        sh *= 2
    return x


def _mlstm_kernel(qk_ref, v_ref, gt_ref, fb_ref, c0_ref, n0_ref, m0_ref,
                  h_ref, c1_ref, n1_ref, m1_ref, c_s, n_s, m_s, *, reverse, nc):
    L = ML_CHUNK
    step = pl.program_id(1)

    @pl.when(step == 0)
    def _load_state():
        c_s[...] = c0_ref[0]
        n_s[...] = n0_ref[0]
        m_s[...] = m0_ref[0]

    gates = gt_ref[0]
    lane = lax.broadcasted_iota(jnp.int32, gates.shape, 1)
    is_forget = (lane & ML_HEADS) != 0
    g2 = jnp.where(is_forget, _log_sigmoid(gates + fb_ref[...]), gates)
    bc = _cumsum_rows(jnp.where(is_forget, g2, 0.0), reverse)
    g2_t = g2.T
    bc_t = bc.T
    ti = lax.broadcasted_iota(jnp.int32, (L, L), 0)
    si = lax.broadcasted_iota(jnp.int32, (L, L), 1)
    causal = (ti <= si) if reverse else (ti >= si)
    last = 0 if reverse else L - 1
    base = 2 * ML_HEADS if reverse else 0
    for hd in range(ML_HEADS):
        li, lf = base + hd, base + ML_HEADS + hd
        q = qk_ref[0, :, ML_DH * hd:ML_DH * (hd + 1)]
        k = qk_ref[0, :, ML_WIDTH + ML_DH * hd:ML_WIDTH + ML_DH * (hd + 1)]
        v = v_ref[0, :, ML_DH * hd:ML_DH * (hd + 1)]
        li_col, li_row = g2[:, li:li + 1], g2_t[li:li + 1, :]
        bc_col, bc_row = bc[:, lf:lf + 1], bc_t[lf:lf + 1, :]
        btot = bc_col[last:last + 1, :]
        m_old = m_s[hd][:, 0:1]
        c_old = c_s[hd]
        n_old = n_s[hd]
        g_row = btot - bc_row + li_row
        m_new = jnp.maximum(btot + m_old, jnp.max(g_row, axis=-1, keepdims=True))
        w_s = jnp.exp(btot - bc_col + li_col - m_new)
        decay = jnp.exp(btot + m_old - m_new)
        kw = k.astype(F32) * w_s
        c_s[hd] = decay * c_old + _dot(kw.T.astype(BF16), v)
        n_s[hd] = decay * n_old + jnp.sum(kw, axis=0, keepdims=True)
        m_s[hd] = jnp.broadcast_to(m_new, (1, LANES))

        dmat = jnp.where(causal, bc_col - bc_row + li_row, MASK_VALUE)
        inter = bc_col + m_old
        m_t = jnp.maximum(inter, jnp.max(dmat, axis=-1, keepdims=True))
        qk = lax.dot_general(q, k, (((1,), (1,)), ((), ())), preferred_element_type=F32)
        w_ts = jnp.exp(dmat - m_t) * qk
        w_c = jnp.exp(inter - m_t)
        num = _dot(w_ts.astype(BF16), v) + w_c * _dot(q, c_old.astype(BF16))
        den = jnp.sum(w_ts, axis=-1, keepdims=True) + w_c * jnp.sum(q.astype(F32) * n_old, axis=-1, keepdims=True)
        h_ref[0, :, ML_DH * hd:ML_DH * (hd + 1)] = num / jnp.maximum(jnp.abs(den), jnp.exp(-m_t))

    @pl.when(step == nc - 1)
    def _store_state():
        c1_ref[0] = c_s[...]
        n1_ref[0] = n_s[...]
        m1_ref[0] = m_s[...]


def _mlstm(qk, v, gates, fb_lane, state, reverse):
    bsz, n, _ = qk.shape
    nc = n // ML_CHUNK
    chunk = (lambda b_, s: (b_, nc - 1 - s, 0)) if reverse else (lambda b_, s: (b_, s, 0))
    st4 = lambda b_, s: (b_, 0, 0, 0)
    c0, n0, m0 = state
    shapes = [(ML_HEADS, ML_DH, ML_DH), (ML_HEADS, 1, ML_DH), (ML_HEADS, 1, LANES)]
    out = pl.pallas_call(
        functools.partial(_mlstm_kernel, reverse=reverse, nc=nc),
        grid=(bsz, nc),
        in_specs=[pl.BlockSpec((1, ML_CHUNK, 2 * ML_WIDTH), chunk),
                  pl.BlockSpec((1, ML_CHUNK, ML_WIDTH), chunk),
                  pl.BlockSpec((1, ML_CHUNK, LANES), chunk),
                  pl.BlockSpec((1, LANES), lambda b_, s: (0, 0))]
                 + [pl.BlockSpec((1,) + sh, st4) for sh in shapes],
        out_specs=[pl.BlockSpec((1, ML_CHUNK, ML_WIDTH), chunk)] + [pl.BlockSpec((1,) + sh, st4) for sh in shapes],
        out_shape=[jax.ShapeDtypeStruct((bsz, n, ML_WIDTH), F32)]
                  + [jax.ShapeDtypeStruct((bsz,) + sh, F32) for sh in shapes],
        scratch_shapes=[pltpu.VMEM(sh, F32) for sh in shapes],
        compiler_params=_cparams(("parallel", "arbitrary")),
        name="mlstm_bwd" if reverse else "mlstm_fwd",
    )(qk, v, gates, fb_lane, c0, n0, m0)
    return out[0], tuple(out[1:])


def _na_kernel(*refs, n_blocks, rows, qb):
    if n_blocks:
        q_ref, kc_ref, vc_ref, kp_ref, k0_ref, k1_ref, vp_ref, v0_ref, v1_ref, bias_ref, o_ref = refs
        k_lat, v_lat = (kp_ref, k0_ref, k1_ref), (vp_ref, v0_ref, v1_ref)
    else:
        q_ref, kc_ref, vc_ref, o_ref = refs
        k_lat, v_lat = (), ()
    i = pl.program_id(1)
    rpb = qb // GRID_W
    if n_blocks:
        qi = lax.broadcasted_iota(jnp.int32, (qb, n_blocks * qb), 0)
        ki = lax.broadcasted_iota(jnp.int32, (qb, n_blocks * qb), 1)
        q_row = rpb * i + (qi >> 6)
        k_row = rpb * (i - 1) + (ki >> 6)
        start = jnp.clip(q_row - NA_ROWS // 2, 0, rows - NA_ROWS)
        row_ok = (k_row >= start) & (k_row < start + NA_ROWS)
    lane = lax.broadcasted_iota(jnp.int32, (qb, LANES), 1)
    nt = (((1,), (1,)), ((), ()))
    for g in range(NA_HEADS // 2):
        sl = slice(LANES * g, LANES * (g + 1))
        q2 = q_ref[0, :, sl]
        outs = []
        for half in range(2):
            hd = 2 * g + half
            q = jnp.where((lane >> 6) == half, q2, jnp.zeros_like(q2))
            s_c = lax.dot_general(q, kc_ref[0, :, sl], nt, preferred_element_type=F32)
            mx = jnp.max(s_c, axis=-1, keepdims=True)
            if n_blocks:
                s_l = jnp.concatenate([lax.dot_general(q, kr[0, :, sl], nt, preferred_element_type=F32)
                                       for kr in k_lat], axis=-1)
                s_l = jnp.where(row_ok, s_l + bias_ref[hd].astype(F32), MASK_VALUE)
                mx = jnp.maximum(mx, jnp.max(s_l, axis=-1, keepdims=True))
                p_l = jnp.exp(s_l - mx)
            p_c = jnp.exp(s_c - mx)
            den = jnp.sum(p_c, axis=-1, keepdims=True)
            acc = _dot(p_c.astype(BF16), vc_ref[0, :, sl])
            if n_blocks:
                den = den + jnp.sum(p_l, axis=-1, keepdims=True)
                for t, vr in enumerate(v_lat):
                    acc = acc + _dot(p_l[:, qb * t:qb * (t + 1)].astype(BF16), vr[0, :, sl])
            outs.append(acc / den)
        o_ref[0, :, sl] = jnp.where((lane >> 6) == 0, outs[0], outs[1]).astype(BF16)


def _na_attn(qn, kn, vn, kc, vc, bias, rows, qb):
    bsz, n, w = qn.shape
    t = kc.shape[1]
    n_blocks = 0 if bias is None else 3
    nb = n // qb
    tok = pl.BlockSpec((1, qb, w), lambda b_, i: (b_, i, 0))
    ctx = pl.BlockSpec((1, t, w), lambda b_, i: (b_, 0, 0))
    ins, specs = [qn, kc, vc], [tok, ctx, ctx]
    if n_blocks:
        prev = pl.BlockSpec((1, qb, w), lambda b_, i: (b_, jnp.maximum(i - 1, 0), 0))
        nxt = pl.BlockSpec((1, qb, w), lambda b_, i: (b_, jnp.minimum(i + 1, nb - 1), 0))
        ins += [kn, kn, kn, vn, vn, vn, bias]
        specs += [prev, tok, nxt, prev, tok, nxt, pl.BlockSpec(bias.shape, lambda b_, i: (0, 0, 0))]
    return pl.pallas_call(
        functools.partial(_na_kernel, n_blocks=n_blocks, rows=rows, qb=qb),
        grid=(bsz, nb),
        in_specs=specs,
        out_specs=tok,
        out_shape=jax.ShapeDtypeStruct((bsz, n, w), BF16),
        compiler_params=_cparams(("parallel", "parallel")),
        name="na_attn" if n_blocks else "ctx_attn",
    )(*ins)


def _take_cols(w, idx):
    idx = np.asarray(idx)
    cols = jnp.take(w, jnp.asarray(np.maximum(idx, 0)), axis=-1)
    return jnp.where(jnp.asarray(idx >= 0), cols, jnp.zeros_like(cols))


def _even_col_index():
    idx = []
    half = DA_QK // 2
    for o in (0, 256, 512, 768):
        for part in (0, half):
            idx += [o + DA_QK * hd + part + i for hd in range(DA_HEADS) for i in range(half)]
    idx += list(range(1024, 1536)) + list(range(1536, 1792)) + list(range(1792, 1920))
    kr, hr = 1920, MLA_ROPE // 2
    idx += [kr + i for i in range(MLA_ROPE)] + [-1] * (LANES - MLA_ROPE)
    idx += [kr + hr + i for i in range(hr)] + [kr + i for i in range(hr)] + [-1] * (LANES - MLA_ROPE)
    idx += list(range(1952, 2976))
    assert len(idx) == EV_COLS
    return idx


def _uq_col_index():
    per = MLA_NOPE + MLA_ROPE
    hr = MLA_ROPE // 2
    idx = [per * hd + i for hd in range(MLA_HEADS) for i in range(MLA_NOPE)]
    idx += [per * hd + MLA_NOPE + i for hd in range(MLA_HEADS) for i in range(hr)]
    idx += [per * hd + MLA_NOPE + hr + i for hd in range(MLA_HEADS) for i in range(hr)]
    return idx


def _ukv_col_index():
    per = MLA_NOPE + MLA_V
    idx = [per * hd + i for hd in range(MLA_HEADS) for i in range(MLA_NOPE)]
    idx += [per * hd + MLA_NOPE + i for hd in range(MLA_HEADS) for i in range(MLA_V)]
    return idx


def _odd_col_index():
    idx = list(range(0, 2048)) + list(range(2048, 2064)) + [-1] * (LANES - 16) + list(range(2064, 4624))
    assert len(idx) == OD_COLS
    return idx


def _rope_tables(n_lat, n_ctx):
    t = jnp.arange(n_lat)
    row = (t // GRID_W).astype(F32)
    col = (t % GRID_W).astype(F32)

    def cs(dim):
        nf = dim // 4
        freqs = ROPE_BASE ** (-jnp.arange(nf, dtype=F32) / nf)
        ang = jnp.concatenate([row[:, None] * freqs, col[:, None] * freqs], axis=-1)
        return jnp.cos(ang), jnp.sin(ang)

    ca, sa = cs(DA_QK)
    cm, sm = cs(MLA_ROPE)
    lat = jnp.concatenate([jnp.tile(ca, (1, 4)), jnp.tile(sa, (1, 4)), jnp.tile(cm, (1, 8)), jnp.tile(sm, (1, 8)),
                           jnp.tile(jnp.concatenate([-sm, sm], axis=-1), (1, 4))], axis=-1)
    one, zero = jnp.ones((n_ctx, LANES), F32), jnp.zeros((n_ctx, LANES), F32)
    return lat, jnp.concatenate([one, zero, one, zero, zero], axis=-1)


def _na_bias_tiles(rpb, qb):
    rows_per = qb // GRID_W
    q = np.arange(qb)
    k = np.arange(3 * qb)
    dr = (k[None, :] // GRID_W - rows_per) - (q[:, None] // GRID_W)
    qc, kc = q[:, None] % GRID_W, k[None, :] % GRID_W
    c0 = np.clip(qc - NA_COLS // 2, 0, GRID_W - NA_COLS)
    ok = (np.abs(dr) <= NA_ROWS - 1) & (kc >= c0) & (kc < c0 + NA_COLS)
    ri = np.clip(dr + NA_ROWS - 1, 0, 2 * NA_ROWS - 2)
    ci = np.clip(kc - qc + NA_COLS - 1, 0, 2 * NA_COLS - 2)
    vals = rpb[:, jnp.asarray(ri), jnp.asarray(ci)]
    return jnp.where(jnp.asarray(ok)[None], vals, MASK_VALUE).astype(F32)


def kernel(x, c, ctx, c_ctx, ada_w, ada_b, ln_g, ln_b, ev_w_in, ev_b_in, da_lambda, da_subln_g, mla_q_norm_g, mla_kv_norm_g, mla_w_uq, mla_w_ukv, ev_w_out, od_w_in, od_b_in, ml_conv_w, ml_conv_b, ml_f_bias, ml_norm_g, na_rpb, od_w_out):
    bsz, n_lat, d = x.shape
    n_ctx = ctx.shape[1]
    rows = n_lat // GRID_W
    assert d == D_MODEL and bsz + 1 <= 8 and n_lat % 512 == 0 and n_ctx % 256 == 0 and rows >= NA_ROWS

    cond8 = jnp.zeros((8, d), F32).at[:bsz].set(c).at[bsz].set(c_ctx)
    mod_all = _ada_mod(cond8, ada_w, ada_b)
    rope_lat, rope_ctx = _rope_tables(n_lat, n_ctx)

    tm_lat, tm_ctx = 256, 256
    tq, tk = 512, 512
    na_qb = 256
    xl, xc = x, ctx
    for l in range(DEPTH):
        update_ctx = l < DEPTH - 1
        i = l // 2
        mod_l = mod_all[l, :bsz].reshape(bsz, 1, 3 * d)
        mod_c = mod_all[l, bsz].reshape(1, 1, 3 * d)
        lng, lnb = ln_g[l].reshape(1, d), ln_b[l].reshape(1, d)
        if l % 2 == 0:
            lam_init = 0.8 - 0.6 * math.exp(-0.3 * l)
            w = _take_cols(ev_w_in[i], _even_col_index()).astype(BF16)
            b = _take_cols(ev_b_in[i], _even_col_index()).reshape(1, EV_COLS)
            wuq = _take_cols(mla_w_uq[i], _uq_col_index()).astype(BF16)
            wukv = _take_cols(mla_w_ukv[i], _ukv_col_index()).astype(BF16)
            qg = mla_q_norm_g[i].reshape(1, MLA_Q_RANK)
            kvg = mla_kv_norm_g[i].reshape(1, MLA_KV_RANK)
            subg = da_subln_g[i].reshape(DA_V, 1)
            wout = ev_w_out[i].astype(BF16)
            qt_l, ka_l, km_l, vt_l, g_l = _proj_even(xl, mod_l, rope_lat, w, b, qg, kvg, wuq, wukv, tm_lat)
            qt_c, ka_c, km_c, vt_c, g_c = _proj_even(xc, mod_c, rope_ctx, w, b, qg, kvg, wuq, wukv, tm_ctx)
            y_l = _attn_even(da_lambda[i], subg, qt_l, ka_l, km_l, vt_l, (ka_c, km_c, vt_c), lam_init, tq, tk)
            xl_new = _out_proj(_out_even_kernel, "out_even", (y_l, g_l), xl, mod_l, wout, (), lng, lnb, tm_lat)
            if update_ctx:
                y_c = _attn_even(da_lambda[i], subg, qt_c, ka_c, km_c, vt_c, None, lam_init, n_ctx, n_ctx)
                xc = _out_proj(_out_even_kernel, "out_even", (y_c, g_c), xc, mod_c, wout, (), lng, lnb, tm_ctx)
            xl = xl_new
        else:
            w = _take_cols(od_w_in[i], _odd_col_index()).astype(BF16)
            b = _take_cols(od_b_in[i], _odd_col_index()).reshape(1, OD_COLS)
            wout = od_w_out[i].astype(BF16)
            cw, cb = ml_conv_w[i], ml_conv_b[i].reshape(1, 2 * ML_WIDTH)
            fb = ml_f_bias[i].astype(F32)
            fb_lane = jnp.zeros((1, LANES), F32).at[0, 4:8].set(fb[0]).at[0, 12:16].set(fb[1])
            ng = ml_norm_g[i].reshape(1, ML_WIDTH)
            qk_l, v_l, so_l, gt_l, qn_l, kn_l, vn_l, g_l = _proj_odd(xl, mod_l, w, b, tm_lat)
            qk_c, v_c, so_c, gt_c, qn_c, kn_c, vn_c, g_c = _proj_odd(xc, mod_c, w, b, tm_ctx)
            qkc_l = _conv_silu(qk_l, cw, cb, 512)
            qkc_c = _conv_silu(qk_c, cw, cb, n_ctx)
            zero = (jnp.zeros((bsz, ML_HEADS, ML_DH, ML_DH), F32), jnp.zeros((bsz, ML_HEADS, 1, ML_DH), F32),
                    jnp.zeros((bsz, ML_HEADS, 1, LANES), F32))
            hf_c, st_f = _mlstm(qkc_c, v_c, gt_c, fb_lane, zero, False)
            hb_c, st_b = _mlstm(qkc_c, v_c, gt_c, fb_lane, zero, True)
            hf_l, _ = _mlstm(qkc_l, v_l, gt_l, fb_lane, st_f, False)
            hb_l, _ = _mlstm(qkc_l, v_l, gt_l, fb_lane, st_b, True)
            bias = _na_bias_tiles(na_rpb[i].astype(F32), na_qb)
            na_l = _na_attn(qn_l, kn_l, vn_l, kn_c, vn_c, bias, rows, na_qb)
            xl_new = _out_proj(_out_odd_kernel, "out_odd", (hf_l, hb_l, so_l, na_l, g_l), xl, mod_l, wout, (ng,), lng, lnb, tm_lat)
            if update_ctx:
                na_c = _na_attn(qn_c, None, None, kn_c, vn_c, None, rows, n_ctx)
                xc = _out_proj(_out_odd_kernel, "out_odd", (hf_c, hb_c, so_c, na_c, g_c), xc, mod_c, wout, (ng,), lng, lnb, tm_ctx)
            xl = xl_new
    return xl
```

```python
import functools
import math

import numpy as np
import jax
import jax.numpy as jnp
from jax import lax
from jax.experimental import pallas as pl
from jax.experimental.pallas import tpu as pltpu

F32 = jnp.float32
BF16 = jnp.bfloat16

D_MODEL = 1024
DEPTH = 4
GRID_W = 64
ROPE_BASE = 10000.0
LN_EPS = 1e-5
RMS_EPS = 1e-6
DEEPNORM_ALPHA = (2.0 * DEPTH) ** 0.25

DA_HEADS, DA_QK, DA_V = 4, 64, 128
LOG2E = math.log2(math.e)
DA_SCALE = DA_QK ** -0.5 * LOG2E
MLA_HEADS, MLA_Q_RANK, MLA_KV_RANK, MLA_NOPE, MLA_ROPE, MLA_V = 8, 256, 128, 64, 32, 64
MLA_SCALE = (MLA_NOPE + MLA_ROPE) ** -0.5 * LOG2E
ML_HEADS, ML_DH, ML_CONV, ML_CHUNK = 4, 128, 5, 128
ML_WIDTH = ML_HEADS * ML_DH
NA_HEADS, NA_DH, NA_ROWS, NA_COLS = 8, 64, 8, 16
NA_WIDTH = NA_HEADS * NA_DH
NA_SCALE = NA_DH ** -0.5

LANES = 128
MASK_VALUE = -1e30
VMEM_LIMIT = 48 * 1024 * 1024

EV_COLS = 3200
EV_V, EV_CQ, EV_CKV, EV_KRA, EV_KRB, EV_GATE = 1024, 1536, 1792, 1920, 2048, 2176
OD_COLS = 4736
OD_V, OD_O, OD_GATES, OD_QN, OD_KN, OD_VN, OD_GATE = 1024, 1536, 2048, 2176, 2688, 3200, 3712
QT_ROWS = 1280
N_UNITS = 16


def _cparams(sem):
    return pltpu.CompilerParams(dimension_semantics=sem, vmem_limit_bytes=VMEM_LIMIT)


def _silu(x):
    return x * jax.nn.sigmoid(x)


def _dot(a, b):
    return jnp.dot(a, b, preferred_element_type=F32)


def _log_sigmoid(x):
    return jnp.minimum(x, 0.0) - jnp.log(1.0 + jnp.exp(-jnp.abs(x)))


def _ada_kernel(c_ref, w_ref, b_ref, o_ref):
    c = c_ref[...]
    o_ref[0] = _dot(_silu(c).astype(BF16), w_ref[0].astype(BF16)) + b_ref[0]


def _ada_mod(cond8, ada_w, ada_b):
    depth, d, e = ada_w.shape
    tn = 1024
    return pl.pallas_call(
        _ada_kernel,
        grid=(depth, e // tn),
        in_specs=[pl.BlockSpec((8, d), lambda l, n: (0, 0)),
                  pl.BlockSpec((1, d, tn), lambda l, n: (l, 0, n)),
                  pl.BlockSpec((1, 1, tn), lambda l, n: (l, 0, n))],
        out_specs=pl.BlockSpec((1, 8, tn), lambda l, n: (l, 0, n)),
        out_shape=jax.ShapeDtypeStruct((depth, 8, e), F32),
        compiler_params=_cparams(("parallel", "parallel")),
        name="ada_mod",
    )(cond8, ada_w, ada_b.reshape(depth, 1, e))


def _mod_spec(mod):
    per_batch = mod.shape[0] > 1
    return pl.BlockSpec((1, 1, mod.shape[2]), (lambda b, i: (b, 0, 0)) if per_batch else (lambda b, i: (0, 0, 0)))


def _proj_even_kernel(x_ref, mod_ref, rope_ref, w_ref, b_ref, qg_ref, kvg_ref, wuq_ref, wukv_ref,
                      qt_ref, ka_ref, km_ref, vt_ref, g_ref):
    d = D_MODEL
    x = x_ref[0]
    mod = mod_ref[0]
    h = (x * (1.0 + mod[:, d:2 * d]) + mod[:, 0:d]).astype(BF16)
    z = _dot(h, w_ref[...]) + b_ref[...]
    rope = rope_ref[...]
    cos_a, sin_a, cos_m, sin_m, sin_ms = (rope[:, LANES * i:LANES * (i + 1)] for i in range(5))

    def blk(o):
        return z[:, o:o + LANES]

    def rot(a, b, c, s):
        return a * c - b * s, b * c + a * s

    q1a, q1b = rot(blk(0), blk(128), cos_a, sin_a)
    q2a, q2b = rot(blk(256), blk(384), cos_a, sin_a)
    k1a, k1b = rot(blk(512), blk(640), cos_a, sin_a)
    k2a, k2b = rot(blk(768), blk(896), cos_a, sin_a)
    for i, t in enumerate((q1a, q1b, q2a, q2b)):
        qt_ref[0, LANES * i:LANES * (i + 1), :] = (t * DA_SCALE).T.astype(BF16)
    for i, t in enumerate((k1a, k1b, k2a, k2b)):
        ka_ref[0, i // 2, :, LANES * (i % 2):LANES * (i % 2 + 1)] = t.astype(BF16)
    for i in range(4):
        vt_ref[0, LANES * i:LANES * (i + 1), :] = blk(EV_V + LANES * i).T.astype(BF16)

    def rms(t, g):
        return (t * lax.rsqrt(jnp.mean(t * t, axis=-1, keepdims=True) + RMS_EPS) * g).astype(BF16)

    qm = _dot(rms(z[:, EV_CQ:EV_CQ + MLA_Q_RANK], qg_ref[...]), wuq_ref[...])
    qra, qrb = rot(qm[:, 512:640], qm[:, 640:768], cos_m, sin_m)
    for i in range(4):
        qt_ref[0, 512 + LANES * i:512 + LANES * (i + 1), :] = (qm[:, LANES * i:LANES * (i + 1)] * MLA_SCALE).T.astype(BF16)
    qt_ref[0, 1024:1152, :] = (qra * MLA_SCALE).T.astype(BF16)
    qt_ref[0, 1152:1280, :] = (qrb * MLA_SCALE).T.astype(BF16)

    kvm = _dot(rms(z[:, EV_CKV:EV_CKV + MLA_KV_RANK], kvg_ref[...]), wukv_ref[...])
    k_rope = (blk(EV_KRA) * cos_m + blk(EV_KRB) * sin_ms).astype(BF16)
    for g in range(4):
        km_ref[0, g, :, 0:LANES] = kvm[:, LANES * g:LANES * (g + 1)].astype(BF16)
        km_ref[0, g, :, LANES:2 * LANES] = k_rope
        vt_ref[0, 512 + LANES * g:512 + LANES * (g + 1), :] = kvm[:, 512 + LANES * g:512 + LANES * (g + 1)].T.astype(BF16)

    g_ref[0] = _silu(z[:, EV_GATE:EV_GATE + d]).astype(BF16)


def _proj_even(x, mod, rope, w, b, qg, kvg, wuq, wukv, tm):
    bsz, n, d = x.shape
    const = lambda shape: pl.BlockSpec(shape, lambda b_, i: (0,) * len(shape))
    return pl.pallas_call(
        _proj_even_kernel,
        grid=(bsz, n // tm),
        in_specs=[pl.BlockSpec((1, tm, d), lambda b_, i: (b_, i, 0)),
                  _mod_spec(mod),
                  pl.BlockSpec((tm, 5 * LANES), lambda b_, i: (i, 0)),
                  const((d, EV_COLS)), const((1, EV_COLS)), const((1, MLA_Q_RANK)), const((1, MLA_KV_RANK)),
                  const((MLA_Q_RANK, 768)), const((MLA_KV_RANK, 1024))],
        out_specs=[pl.BlockSpec((1, QT_ROWS, tm), lambda b_, i: (b_, 0, i)),
                   pl.BlockSpec((1, 2, tm, 256), lambda b_, i: (b_, 0, i, 0)),
                   pl.BlockSpec((1, 4, tm, 256), lambda b_, i: (b_, 0, i, 0)),
                   pl.BlockSpec((1, 1024, tm), lambda b_, i: (b_, 0, i)),
                   pl.BlockSpec((1, tm, d), lambda b_, i: (b_, i, 0))],
        out_shape=[jax.ShapeDtypeStruct((bsz, QT_ROWS, n), BF16),
                   jax.ShapeDtypeStruct((bsz, 2, n, 256), BF16),
                   jax.ShapeDtypeStruct((bsz, 4, n, 256), BF16),
                   jax.ShapeDtypeStruct((bsz, 1024, n), BF16),
                   jax.ShapeDtypeStruct((bsz, n, d), BF16)],
        compiler_params=_cparams(("parallel", "parallel")),
        name="proj_even",
    )(x, mod, rope, w, b, qg, kvg, wuq, wukv)


def _attn_even_kernel(*refs, tq, nkv, has_ctx, lam_init):
    if has_ctx:
        (lam_ref, subg_ref, qt_ref, ka_ref, km_ref, vt_ref, kac_ref, kmc_ref, vtc_ref,
         y_ref, qw, acc_a, acc_m, m_s, l_s) = refs
    else:
        (lam_ref, subg_ref, qt_ref, ka_ref, km_ref, vt_ref,
         y_ref, qw, acc_a, acc_m, m_s, l_s) = refs
    j = pl.program_id(2)

    def process(ka, km, vt):
        def scores(u):
            k = ka[0, u // DA_HEADS] if u < 2 * DA_HEADS else km[0, (u - 2 * DA_HEADS) // 2]
            return _dot(k, qw[u])

        def softmax_step(u, s):
            m_old = m_s[u]
            m_new = jnp.maximum(m_old, jnp.max(s, axis=0, keepdims=True))
            alpha = jnp.exp2(m_old - m_new)
            p = jnp.exp2(s - m_new)
            l_s[u] = alpha * l_s[u] + jnp.sum(p, axis=0, keepdims=True)
            m_s[u] = m_new
            return alpha, p.astype(BF16)

        s_next = scores(0)
        for u in range(N_UNITS):
            s_cur = s_next
            if u + 1 < N_UNITS:
                s_next = scores(u + 1)
            alpha, p = softmax_step(u, s_cur)
            if u < 2 * DA_HEADS:
                hd = u % DA_HEADS
                acc_a[u] = alpha * acc_a[u] + _dot(vt[0, DA_V * hd:DA_V * (hd + 1), :], p)
            else:
                hd = u - 2 * DA_HEADS
                v = vt[0, DA_HEADS * DA_V + MLA_V * hd:DA_HEADS * DA_V + MLA_V * (hd + 1), :]
                acc_m[hd] = alpha * acc_m[hd] + _dot(v, p)

    @pl.when(j == 0)
    def _init():
        row = lax.broadcasted_iota(jnp.int32, (256, tq), 0)
        head_of_row = (row >> 5) & (DA_HEADS - 1)
        for mp in range(2):
            q = qt_ref[0, 256 * mp:256 * (mp + 1), :]
            for hd in range(DA_HEADS):
                qw[mp * DA_HEADS + hd] = jnp.where(head_of_row == hd, q, jnp.zeros_like(q))
        for hd in range(MLA_HEADS):
            u = 2 * DA_HEADS + hd
            qw[u] = jnp.zeros((256, tq), BF16)
            qw[u, 64 * (hd % 2):64 * (hd % 2) + 64, :] = qt_ref[0, 512 + 64 * hd:512 + 64 * hd + 64, :]
            qw[u, 128:144, :] = qt_ref[0, 1024 + 16 * hd:1024 + 16 * hd + 16, :]
            qw[u, 144:160, :] = qt_ref[0, 1152 + 16 * hd:1152 + 16 * hd + 16, :]
        acc_a[...] = jnp.zeros(acc_a.shape, F32)
        acc_m[...] = jnp.zeros(acc_m.shape, F32)
        m_s[...] = jnp.full(m_s.shape, MASK_VALUE, F32)
        l_s[...] = jnp.zeros(l_s.shape, F32)
        if has_ctx:
            process(kac_ref, kmc_ref, vtc_ref)

    process(ka_ref, km_ref, vt_ref)

    @pl.when(j == nkv - 1)
    def _finish():
        lm = lam_ref[...]
        lam = (jnp.exp(jnp.sum(lm[0:1] * lm[1:2], axis=-1, keepdims=True))
               - jnp.exp(jnp.sum(lm[2:3] * lm[3:4], axis=-1, keepdims=True)) + lam_init)
        for hd in range(DA_HEADS):
            o = acc_a[hd] / l_s[hd] - lam * (acc_a[DA_HEADS + hd] / l_s[DA_HEADS + hd])
            o = o * lax.rsqrt(jnp.mean(o * o, axis=0, keepdims=True) + RMS_EPS) * subg_ref[...] * (1.0 - lam_init)
            y_ref[0, :, DA_V * hd:DA_V * (hd + 1)] = o.T.astype(BF16)
        for g in range(MLA_HEADS // 2):
            o = jnp.concatenate([acc_m[2 * g] / l_s[2 * DA_HEADS + 2 * g],
                                 acc_m[2 * g + 1] / l_s[2 * DA_HEADS + 2 * g + 1]], axis=0)
            y_ref[0, :, 512 + LANES * g:512 + LANES * (g + 1)] = o.T.astype(BF16)


def _attn_even(lam_p, subg, qt, ka, km, vt, ctx_kv, lam_init, tq, tk):
    bsz, _, n = qt.shape
    nk = ka.shape[2]
    nq, nkv = n // tq, nk // tk
    has_ctx = ctx_kv is not None
    in_specs = [pl.BlockSpec((4, DA_QK), lambda b, i, j: (0, 0)),
                pl.BlockSpec((DA_V, 1), lambda b, i, j: (0, 0)),
                pl.BlockSpec((1, QT_ROWS, tq), lambda b, i, j: (b, 0, i)),
                pl.BlockSpec((1, 2, tk, 256), lambda b, i, j: (b, 0, j, 0)),
                pl.BlockSpec((1, 4, tk, 256), lambda b, i, j: (b, 0, j, 0)),
                pl.BlockSpec((1, 1024, tk), lambda b, i, j: (b, 0, j))]
    args = [lam_p, subg, qt, ka, km, vt]
    if has_ctx:
        tc = ctx_kv[0].shape[2]
        in_specs += [pl.BlockSpec((1, 2, tc, 256), lambda b, i, j: (b, 0, 0, 0)),
                     pl.BlockSpec((1, 4, tc, 256), lambda b, i, j: (b, 0, 0, 0)),
                     pl.BlockSpec((1, 1024, tc), lambda b, i, j: (b, 0, 0))]
        args += list(ctx_kv)
    return pl.pallas_call(
        functools.partial(_attn_even_kernel, tq=tq, nkv=nkv, has_ctx=has_ctx, lam_init=lam_init),
        grid=(bsz, nq, nkv),
        in_specs=in_specs,
        out_specs=pl.BlockSpec((1, tq, D_MODEL), lambda b, i, j: (b, i, 0)),
        out_shape=jax.ShapeDtypeStruct((bsz, n, D_MODEL), BF16),
        scratch_shapes=[pltpu.VMEM((N_UNITS, 256, tq), BF16),
                        pltpu.VMEM((2 * DA_HEADS, DA_V, tq), F32),
                        pltpu.VMEM((MLA_HEADS, MLA_V, tq), F32),
                        pltpu.VMEM((N_UNITS, 1, tq), F32),
                        pltpu.VMEM((N_UNITS, 1, tq), F32)],
        compiler_params=_cparams(("parallel", "parallel", "arbitrary")),
        name="attn_even",
    )(*args)


def _layer_norm_rows(t, g, b):
    mu = jnp.mean(t, axis=-1, keepdims=True)
    tc = t - mu
    var = jnp.mean(tc * tc, axis=-1, keepdims=True)
    return tc * lax.rsqrt(var + LN_EPS) * g + b


def _out_even_kernel(y_ref, g_ref, x_ref, mod_ref, w_ref, lng_ref, lnb_ref, o_ref):
    d = D_MODEL
    y = (y_ref[0].astype(F32) * g_ref[0].astype(F32)).astype(BF16)
    out = _dot(y, w_ref[...])
    t = DEEPNORM_ALPHA * x_ref[0] + mod_ref[0][:, 2 * d:3 * d] * out
    o_ref[0] = _layer_norm_rows(t, lng_ref[...], lnb_ref[...])


def _out_odd_kernel(hf_ref, hb_ref, so_ref, na_ref, g_ref, x_ref, mod_ref, w_ref, ng_ref, lng_ref, lnb_ref, o_ref):
    d = D_MODEL
    hsum = hf_ref[0] + hb_ref[0]
    parts = []
    for hd in range(ML_HEADS):
        t = hsum[:, ML_DH * hd:ML_DH * (hd + 1)]
        mu = jnp.mean(t, axis=-1, keepdims=True)
        tc = t - mu
        var = jnp.mean(tc * tc, axis=-1, keepdims=True)
        parts.append(tc * lax.rsqrt(var + LN_EPS))
    y_ml = jnp.concatenate(parts, axis=-1) * ng_ref[...] * so_ref[0].astype(F32)
    g = g_ref[0].astype(F32)
    y = jnp.concatenate([y_ml * g[:, 0:ML_WIDTH], na_ref[0].astype(F32) * g[:, ML_WIDTH:]], axis=-1).astype(BF16)
    out = _dot(y, w_ref[...])
    t = DEEPNORM_ALPHA * x_ref[0] + mod_ref[0][:, 2 * d:3 * d] * out
    o_ref[0] = _layer_norm_rows(t, lng_ref[...], lnb_ref[...])


def _out_proj(kernel_fn, name, acts, x, mod, w, extra, ln_g, ln_b, tm):
    bsz, n, d = x.shape
    tok = lambda a: pl.BlockSpec((1, tm, a.shape[2]), lambda b_, i: (b_, i, 0))
    const = lambda a: pl.BlockSpec(a.shape, lambda b_, i: (0,) * a.ndim)
    ins = list(acts) + [x, mod, w] + list(extra) + [ln_g, ln_b]
    specs = [tok(a) for a in acts] + [tok(x), _mod_spec(mod), const(w)] + [const(e) for e in extra] + [const(ln_g), const(ln_b)]
    return pl.pallas_call(
        kernel_fn,
        grid=(bsz, n // tm),
        in_specs=specs,
        out_specs=pl.BlockSpec((1, tm, d), lambda b_, i: (b_, i, 0)),
        out_shape=jax.ShapeDtypeStruct((bsz, n, d), F32),
        compiler_params=_cparams(("parallel", "parallel")),
        name=name,
    )(*ins)


def _proj_odd_kernel(x_ref, mod_ref, w_ref, b_ref, qk_ref, v_ref, so_ref, gt_ref, qn_ref, kn_ref, vn_ref, g_ref):
    d = D_MODEL
    x = x_ref[0]
    mod = mod_ref[0]
    h = (x * (1.0 + mod[:, d:2 * d]) + mod[:, 0:d]).astype(BF16)
    z = _dot(h, w_ref[...]) + b_ref[...]
    qk_ref[0] = z[:, 0:2 * ML_WIDTH]
    v_ref[0] = z[:, OD_V:OD_V + ML_WIDTH].astype(BF16)
    so_ref[0] = jax.nn.sigmoid(z[:, OD_O:OD_O + ML_WIDTH]).astype(BF16)
    gt_ref[0] = z[:, OD_GATES:OD_GATES + LANES]
    qn_ref[0] = (z[:, OD_QN:OD_QN + NA_WIDTH] * NA_SCALE).astype(BF16)
    kn_ref[0] = z[:, OD_KN:OD_KN + NA_WIDTH].astype(BF16)
    vn_ref[0] = z[:, OD_VN:OD_VN + NA_WIDTH].astype(BF16)
    g_ref[0] = _silu(z[:, OD_GATE:OD_GATE + d]).astype(BF16)


def _proj_odd(x, mod, w, b, tm):
    bsz, n, d = x.shape
    const = lambda shape: pl.BlockSpec(shape, lambda b_, i: (0,) * len(shape))
    widths = [(2 * ML_WIDTH, F32), (ML_WIDTH, BF16), (ML_WIDTH, BF16), (LANES, F32),
              (NA_WIDTH, BF16), (NA_WIDTH, BF16), (NA_WIDTH, BF16), (d, BF16)]
    return pl.pallas_call(
        _proj_odd_kernel,
        grid=(bsz, n // tm),
        in_specs=[pl.BlockSpec((1, tm, d), lambda b_, i: (b_, i, 0)), _mod_spec(mod),
                  const((d, OD_COLS)), const((1, OD_COLS))],
        out_specs=[pl.BlockSpec((1, tm, wd), lambda b_, i: (b_, i, 0)) for wd, _ in widths],
        out_shape=[jax.ShapeDtypeStruct((bsz, n, wd), dt) for wd, dt in widths],
        compiler_params=_cparams(("parallel", "parallel")),
        name="proj_odd",
    )(x, mod, w, b)


def _conv_kernel(cur_ref, prev_ref, next_ref, w_ref, b_ref, o_ref, buf, *, tm, nt):
    i = pl.program_id(1)
    buf[0:8, :] = jnp.where(i > 0, prev_ref[0], 0.0)
    buf[8:8 + tm, :] = cur_ref[0]
    buf[8 + tm:16 + tm, :] = jnp.where(i < nt - 1, next_ref[0], 0.0)
    w = w_ref[...]
    acc = b_ref[...] + buf[6:6 + tm, :] * w[0:1]
    for tap in range(1, ML_CONV):
        acc = acc + buf[6 + tap:6 + tap + tm, :] * w[tap:tap + 1]
    y = _silu(acc)
    o_ref[0, :, 0:ML_WIDTH] = y[:, 0:ML_WIDTH].astype(BF16)
    o_ref[0, :, ML_WIDTH:] = (y[:, ML_WIDTH:] * (ML_DH ** -0.5)).astype(BF16)


def _conv_silu(qk, w, b, tm):
    bsz, n, c = qk.shape
    nt = n // tm
    r = tm // 8
    last8 = n // 8 - 1
    return pl.pallas_call(
        functools.partial(_conv_kernel, tm=tm, nt=nt),
        grid=(bsz, nt),
        in_specs=[pl.BlockSpec((1, tm, c), lambda b_, i: (b_, i, 0)),
                  pl.BlockSpec((1, 8, c), lambda b_, i: (b_, jnp.maximum(i * r - 1, 0), 0)),
                  pl.BlockSpec((1, 8, c), lambda b_, i: (b_, jnp.minimum((i + 1) * r, last8), 0)),
                  pl.BlockSpec((ML_CONV, c), lambda b_, i: (0, 0)),
                  pl.BlockSpec((1, c), lambda b_, i: (0, 0))],
        out_specs=pl.BlockSpec((1, tm, c), lambda b_, i: (b_, i, 0)),
        out_shape=jax.ShapeDtypeStruct((bsz, n, c), BF16),
        scratch_shapes=[pltpu.VMEM((tm + 16, c), F32)],
        compiler_params=_cparams(("parallel", "parallel")),
        name="conv_silu",
    )(qk, qk, qk, w, b)


def _cumsum_rows(x, reverse):
    n = x.shape[0]
    row = lax.broadcasted_iota(jnp.int32, x.shape, 0)
    sh = 1
    while sh < n:
        if reverse:
            x = x + jnp.where(row + sh < n, pltpu.roll(x, n - sh, 0), 0.0)
        else:
            x = x + jnp.where(row >= sh, pltpu.roll(x, sh, 0), 0.0)
        sh *= 2
    return x


def _mlstm_kernel(qk_ref, v_ref, gt_ref, fb_ref, c0_ref, n0_ref, m0_ref,
                  h_ref, c1_ref, n1_ref, m1_ref, c_s, n_s, m_s, *, reverse, nc):
    L = ML_CHUNK
    step = pl.program_id(1)

    @pl.when(step == 0)
    def _load_state():
        c_s[...] = c0_ref[0]
        n_s[...] = n0_ref[0]
        m_s[...] = m0_ref[0]

    gates = gt_ref[0]
    lane = lax.broadcasted_iota(jnp.int32, gates.shape, 1)
    is_forget = (lane & ML_HEADS) != 0
    g2 = jnp.where(is_forget, _log_sigmoid(gates + fb_ref[...]), gates)
    bc = _cumsum_rows(jnp.where(is_forget, g2, 0.0), reverse)
    g2_t = g2.T
    bc_t = bc.T
    ti = lax.broadcasted_iota(jnp.int32, (L, L), 0)
    si = lax.broadcasted_iota(jnp.int32, (L, L), 1)
    causal = (ti <= si) if reverse else (ti >= si)
    last = 0 if reverse else L - 1
    base = 2 * ML_HEADS if reverse else 0
    for hd in range(ML_HEADS):
        li, lf = base + hd, base + ML_HEADS + hd
        q = qk_ref[0, :, ML_DH * hd:ML_DH * (hd + 1)]
        k = qk_ref[0, :, ML_WIDTH + ML_DH * hd:ML_WIDTH + ML_DH * (hd + 1)]
        v = v_ref[0, :, ML_DH * hd:ML_DH * (hd + 1)]
        li_col, li_row = g2[:, li:li + 1], g2_t[li:li + 1, :]
        bc_col, bc_row = bc[:, lf:lf + 1], bc_t[lf:lf + 1, :]
        btot = bc_col[last:last + 1, :]
        m_old = m_s[hd][:, 0:1]
        c_old = c_s[hd]
        n_old = n_s[hd]
        g_row = btot - bc_row + li_row
        m_new = jnp.maximum(btot + m_old, jnp.max(g_row, axis=-1, keepdims=True))
        w_s = jnp.exp(btot - bc_col + li_col - m_new)
        decay = jnp.exp(btot + m_old - m_new)
        kw = k.astype(F32) * w_s
        c_s[hd] = decay * c_old + _dot(kw.T.astype(BF16), v)
        n_s[hd] = decay * n_old + jnp.sum(kw, axis=0, keepdims=True)
        m_s[hd] = jnp.broadcast_to(m_new, (1, LANES))

        dmat = jnp.where(causal, bc_col - bc_row + li_row, MASK_VALUE)
        inter = bc_col + m_old
        m_t = jnp.maximum(inter, jnp.max(dmat, axis=-1, keepdims=True))
        qk = lax.dot_general(q, k, (((1,), (1,)), ((), ())), preferred_element_type=F32)
        w_ts = jnp.exp(dmat - m_t) * qk
        w_c = jnp.exp(inter - m_t)
        num = _dot(w_ts.astype(BF16), v) + w_c * _dot(q, c_old.astype(BF16))
        den = jnp.sum(w_ts, axis=-1, keepdims=True) + w_c * jnp.sum(q.astype(F32) * n_old, axis=-1, keepdims=True)
        h_ref[0, :, ML_DH * hd:ML_DH * (hd + 1)] = num / jnp.maximum(jnp.abs(den), jnp.exp(-m_t))

    @pl.when(step == nc - 1)
    def _store_state():
        c1_ref[0] = c_s[...]
        n1_ref[0] = n_s[...]
        m1_ref[0] = m_s[...]


def _mlstm(qk, v, gates, fb_lane, state, reverse):
    bsz, n, _ = qk.shape
    nc = n // ML_CHUNK
    chunk = (lambda b_, s: (b_, nc - 1 - s, 0)) if reverse else (lambda b_, s: (b_, s, 0))
    st4 = lambda b_, s: (b_, 0, 0, 0)
    c0, n0, m0 = state
    shapes = [(ML_HEADS, ML_DH, ML_DH), (ML_HEADS, 1, ML_DH), (ML_HEADS, 1, LANES)]
    out = pl.pallas_call(
        functools.partial(_mlstm_kernel, reverse=reverse, nc=nc),
        grid=(bsz, nc),
        in_specs=[pl.BlockSpec((1, ML_CHUNK, 2 * ML_WIDTH), chunk),
                  pl.BlockSpec((1, ML_CHUNK, ML_WIDTH), chunk),
                  pl.BlockSpec((1, ML_CHUNK, LANES), chunk),
                  pl.BlockSpec((1, LANES), lambda b_, s: (0, 0))]
                 + [pl.BlockSpec((1,) + sh, st4) for sh in shapes],
        out_specs=[pl.BlockSpec((1, ML_CHUNK, ML_WIDTH), chunk)] + [pl.BlockSpec((1,) + sh, st4) for sh in shapes],
        out_shape=[jax.ShapeDtypeStruct((bsz, n, ML_WIDTH), F32)]
                  + [jax.ShapeDtypeStruct((bsz,) + sh, F32) for sh in shapes],
        scratch_shapes=[pltpu.VMEM(sh, F32) for sh in shapes],
        compiler_params=_cparams(("parallel", "arbitrary")),
        name="mlstm_bwd" if reverse else "mlstm_fwd",
    )(qk, v, gates, fb_lane, c0, n0, m0)
    return out[0], tuple(out[1:])


def _na_kernel(*refs, n_blocks, rows, qb):
    if n_blocks:
        q_ref, kc_ref, vc_ref, kp_ref, k0_ref, k1_ref, vp_ref, v0_ref, v1_ref, bias_ref, o_ref = refs
        k_lat, v_lat = (kp_ref, k0_ref, k1_ref), (vp_ref, v0_ref, v1_ref)
    else:
        q_ref, kc_ref, vc_ref, o_ref = refs
        k_lat, v_lat = (), ()
    i = pl.program_id(1)
    rpb = qb // GRID_W
    if n_blocks:
        qi = lax.broadcasted_iota(jnp.int32, (qb, n_blocks * qb), 0)
        ki = lax.broadcasted_iota(jnp.int32, (qb, n_blocks * qb), 1)
        q_row = rpb * i + (qi >> 6)
        k_row = rpb * (i - 1) + (ki >> 6)
        start = jnp.clip(q_row - NA_ROWS // 2, 0, rows - NA_ROWS)
        row_ok = (k_row >= start) & (k_row < start + NA_ROWS)
    lane = lax.broadcasted_iota(jnp.int32, (qb, LANES), 1)
    nt = (((1,), (1,)), ((), ()))
    for g in range(NA_HEADS // 2):
        sl = slice(LANES * g, LANES * (g + 1))
        q2 = q_ref[0, :, sl]
        outs = []
        for half in range(2):
            hd = 2 * g + half
            q = jnp.where((lane >> 6) == half, q2, jnp.zeros_like(q2))
            s_c = lax.dot_general(q, kc_ref[0, :, sl], nt, preferred_element_type=F32)
            mx = jnp.max(s_c, axis=-1, keepdims=True)
            if n_blocks:
                s_l = jnp.concatenate([lax.dot_general(q, kr[0, :, sl], nt, preferred_element_type=F32)
                                       for kr in k_lat], axis=-1)
                s_l = jnp.where(row_ok, s_l + bias_ref[hd].astype(F32), MASK_VALUE)
                mx = jnp.maximum(mx, jnp.max(s_l, axis=-1, keepdims=True))
                p_l = jnp.exp(s_l - mx)
            p_c = jnp.exp(s_c - mx)
            den = jnp.sum(p_c, axis=-1, keepdims=True)
            acc = _dot(p_c.astype(BF16), vc_ref[0, :, sl])
            if n_blocks:
                den = den + jnp.sum(p_l, axis=-1, keepdims=True)
                for t, vr in enumerate(v_lat):
                    acc = acc + _dot(p_l[:, qb * t:qb * (t + 1)].astype(BF16), vr[0, :, sl])
            outs.append(acc / den)
        o_ref[0, :, sl] = jnp.where((lane >> 6) == 0, outs[0], outs[1]).astype(BF16)


def _na_attn(qn, kn, vn, kc, vc, bias, rows, qb):
    bsz, n, w = qn.shape
    t = kc.shape[1]
    n_blocks = 0 if bias is None else 3
    nb = n // qb
    tok = pl.BlockSpec((1, qb, w), lambda b_, i: (b_, i, 0))
    ctx = pl.BlockSpec((1, t, w), lambda b_, i: (b_, 0, 0))
    ins, specs = [qn, kc, vc], [tok, ctx, ctx]
    if n_blocks:
        prev = pl.BlockSpec((1, qb, w), lambda b_, i: (b_, jnp.maximum(i - 1, 0), 0))
        nxt = pl.BlockSpec((1, qb, w), lambda b_, i: (b_, jnp.minimum(i + 1, nb - 1), 0))
        ins += [kn, kn, kn, vn, vn, vn, bias]
        specs += [prev, tok, nxt, prev, tok, nxt, pl.BlockSpec(bias.shape, lambda b_, i: (0, 0, 0))]
    return pl.pallas_call(
        functools.partial(_na_kernel, n_blocks=n_blocks, rows=rows, qb=qb),
        grid=(bsz, nb),
        in_specs=specs,
        out_specs=tok,
        out_shape=jax.ShapeDtypeStruct((bsz, n, w), BF16),
        compiler_params=_cparams(("parallel", "parallel")),
        name="na_attn" if n_blocks else "ctx_attn",
    )(*ins)


def _take_cols(w, idx):
    idx = np.asarray(idx)
    cols = jnp.take(w, jnp.asarray(np.maximum(idx, 0)), axis=-1)
    return jnp.where(jnp.asarray(idx >= 0), cols, jnp.zeros_like(cols))


def _even_col_index():
    idx = []
    half = DA_QK // 2
    for o in (0, 256, 512, 768):
        for part in (0, half):
            idx += [o + DA_QK * hd + part + i for hd in range(DA_HEADS) for i in range(half)]
    idx += list(range(1024, 1536)) + list(range(1536, 1792)) + list(range(1792, 1920))
    kr, hr = 1920, MLA_ROPE // 2
    idx += [kr + i for i in range(MLA_ROPE)] + [-1] * (LANES - MLA_ROPE)
    idx += [kr + hr + i for i in range(hr)] + [kr + i for i in range(hr)] + [-1] * (LANES - MLA_ROPE)
    idx += list(range(1952, 2976))
    assert len(idx) == EV_COLS
    return idx


def _uq_col_index():
    per = MLA_NOPE + MLA_ROPE
    hr = MLA_ROPE // 2
    idx = [per * hd + i for hd in range(MLA_HEADS) for i in range(MLA_NOPE)]
    idx += [per * hd + MLA_NOPE + i for hd in range(MLA_HEADS) for i in range(hr)]
    idx += [per * hd + MLA_NOPE + hr + i for hd in range(MLA_HEADS) for i in range(hr)]
    return idx


def _ukv_col_index():
    per = MLA_NOPE + MLA_V
    idx = [per * hd + i for hd in range(MLA_HEADS) for i in range(MLA_NOPE)]
    idx += [per * hd + MLA_NOPE + i for hd in range(MLA_HEADS) for i in range(MLA_V)]
    return idx


def _odd_col_index():
    idx = list(range(0, 2048)) + list(range(2048, 2064)) + [-1] * (LANES - 16) + list(range(2064, 4624))
    assert len(idx) == OD_COLS
    return idx


def _rope_tables(n_lat, n_ctx):
    t = jnp.arange(n_lat)
    row = (t // GRID_W).astype(F32)
    col = (t % GRID_W).astype(F32)

    def cs(dim):
        nf = dim // 4
        freqs = ROPE_BASE ** (-jnp.arange(nf, dtype=F32) / nf)
        ang = jnp.concatenate([row[:, None] * freqs, col[:, None] * freqs], axis=-1)
        return jnp.cos(ang), jnp.sin(ang)

    ca, sa = cs(DA_QK)
    cm, sm = cs(MLA_ROPE)
    lat = jnp.concatenate([jnp.tile(ca, (1, 4)), jnp.tile(sa, (1, 4)), jnp.tile(cm, (1, 8)), jnp.tile(sm, (1, 8)),
                           jnp.tile(jnp.concatenate([-sm, sm], axis=-1), (1, 4))], axis=-1)
    one, zero = jnp.ones((n_ctx, LANES), F32), jnp.zeros((n_ctx, LANES), F32)
    return lat, jnp.concatenate([one, zero, one, zero, zero], axis=-1)


def _na_bias_tiles(rpb, qb):
    heads, n_dr, n_dc = rpb.shape
    rows_per = qb // GRID_W
    qc, kc = np.arange(GRID_W)[:, None], np.arange(GRID_W)[None, :]
    c0 = np.clip(qc - NA_COLS // 2, 0, GRID_W - NA_COLS)
    col_ok = (kc >= c0) & (kc < c0 + NA_COLS)
    onehot = ((kc - qc + NA_COLS - 1)[None] == np.arange(n_dc)[:, None, None]) & col_ok[None]
    tiles = jnp.einsum('hrj,jp->hrp', rpb, jnp.asarray(onehot.reshape(n_dc, -1), F32),
                       precision=lax.Precision.HIGHEST).reshape(heads, n_dr, GRID_W, GRID_W)
    tiles = jnp.where(jnp.asarray(col_ok)[None, None], tiles, MASK_VALUE)
    dr = (np.arange(3 * rows_per)[None, :] - rows_per) - np.arange(rows_per)[:, None]
    dr_ok = np.abs(dr) <= NA_ROWS - 1
    idx = np.clip(dr + NA_ROWS - 1, 0, n_dr - 1).reshape(-1)
    blocks = jnp.take(tiles, jnp.asarray(idx), axis=1).reshape(heads, rows_per, 3 * rows_per, GRID_W, GRID_W)
    blocks = jnp.where(jnp.asarray(dr_ok)[None, :, :, None, None], blocks, MASK_VALUE)
    return blocks.transpose(0, 1, 3, 2, 4).reshape(heads, qb, 3 * qb)


def kernel(x, c, ctx, c_ctx, ada_w, ada_b, ln_g, ln_b, ev_w_in, ev_b_in, da_lambda, da_subln_g, mla_q_norm_g, mla_kv_norm_g, mla_w_uq, mla_w_ukv, ev_w_out, od_w_in, od_b_in, ml_conv_w, ml_conv_b, ml_f_bias, ml_norm_g, na_rpb, od_w_out):
    bsz, n_lat, d = x.shape
    n_ctx = ctx.shape[1]
    rows = n_lat // GRID_W
    assert d == D_MODEL and bsz + 1 <= 8 and n_lat % 512 == 0 and n_ctx % 256 == 0 and rows >= NA_ROWS

    cond8 = jnp.zeros((8, d), F32).at[:bsz].set(c).at[bsz].set(c_ctx)
    mod_all = _ada_mod(cond8, ada_w, ada_b)
    rope_lat, rope_ctx = _rope_tables(n_lat, n_ctx)

    tm_lat, tm_ctx = 256, 256
    tq, tk = 512, 512
    na_qb = 256
    xl, xc = x, ctx
    for l in range(DEPTH):
        update_ctx = l < DEPTH - 1
        i = l // 2
        mod_l = mod_all[l, :bsz].reshape(bsz, 1, 3 * d)
        mod_c = mod_all[l, bsz].reshape(1, 1, 3 * d)
        lng, lnb = ln_g[l].reshape(1, d), ln_b[l].reshape(1, d)
        if l % 2 == 0:
            lam_init = 0.8 - 0.6 * math.exp(-0.3 * l)
            w = _take_cols(ev_w_in[i], _even_col_index()).astype(BF16)
            b = _take_cols(ev_b_in[i], _even_col_index()).reshape(1, EV_COLS)
            wuq = _take_cols(mla_w_uq[i], _uq_col_index()).astype(BF16)
            wukv = _take_cols(mla_w_ukv[i], _ukv_col_index()).astype(BF16)
            qg = mla_q_norm_g[i].reshape(1, MLA_Q_RANK)
            kvg = mla_kv_norm_g[i].reshape(1, MLA_KV_RANK)
            subg = da_subln_g[i].reshape(DA_V, 1)
            wout = ev_w_out[i].astype(BF16)
            qt_l, ka_l, km_l, vt_l, g_l = _proj_even(xl, mod_l, rope_lat, w, b, qg, kvg, wuq, wukv, tm_lat)
            qt_c, ka_c, km_c, vt_c, g_c = _proj_even(xc, mod_c, rope_ctx, w, b, qg, kvg, wuq, wukv, tm_ctx)
            y_l = _attn_even(da_lambda[i], subg, qt_l, ka_l, km_l, vt_l, (ka_c, km_c, vt_c), lam_init, tq, tk)
            xl_new = _out_proj(_out_even_kernel, "out_even", (y_l, g_l), xl, mod_l, wout, (), lng, lnb, tm_lat)
            if update_ctx:
                y_c = _attn_even(da_lambda[i], subg, qt_c, ka_c, km_c, vt_c, None, lam_init, n_ctx, n_ctx)
                xc = _out_proj(_out_even_kernel, "out_even", (y_c, g_c), xc, mod_c, wout, (), lng, lnb, tm_ctx)
            xl = xl_new
        else:
            w = _take_cols(od_w_in[i], _odd_col_index()).astype(BF16)
            b = _take_cols(od_b_in[i], _odd_col_index()).reshape(1, OD_COLS)
            wout = od_w_out[i].astype(BF16)
            cw, cb = ml_conv_w[i], ml_conv_b[i].reshape(1, 2 * ML_WIDTH)
            fb = ml_f_bias[i].astype(F32)
            fb_lane = jnp.zeros((1, LANES), F32).at[0, 4:8].set(fb[0]).at[0, 12:16].set(fb[1])
            ng = ml_norm_g[i].reshape(1, ML_WIDTH)
            qk_l, v_l, so_l, gt_l, qn_l, kn_l, vn_l, g_l = _proj_odd(xl, mod_l, w, b, tm_lat)
            qk_c, v_c, so_c, gt_c, qn_c, kn_c, vn_c, g_c = _proj_odd(xc, mod_c, w, b, tm_ctx)
            qkc_l = _conv_silu(qk_l, cw, cb, 512)
            qkc_c = _conv_silu(qk_c, cw, cb, n_ctx)
            zero = (jnp.zeros((bsz, ML_HEADS, ML_DH, ML_DH), F32), jnp.zeros((bsz, ML_HEADS, 1, ML_DH), F32),
                    jnp.zeros((bsz, ML_HEADS, 1, LANES), F32))
            hf_c, st_f = _mlstm(qkc_c, v_c, gt_c, fb_lane, zero, False)
            hb_c, st_b = _mlstm(qkc_c, v_c, gt_c, fb_lane, zero, True)
            hf_l, _ = _mlstm(qkc_l, v_l, gt_l, fb_lane, st_f, False)
            hb_l, _ = _mlstm(qkc_l, v_l, gt_l, fb_lane, st_b, True)
            bias = _na_bias_tiles(na_rpb[i].astype(F32), na_qb)
            na_l = _na_attn(qn_l, kn_l, vn_l, kn_c, vn_c, bias, rows, na_qb)
            xl_new = _out_proj(_out_odd_kernel, "out_odd", (hf_l, hb_l, so_l, na_l, g_l), xl, mod_l, wout, (ng,), lng, lnb, tm_lat)
            if update_ctx:
                na_c = _na_attn(qn_c, None, None, kn_c, vn_c, None, rows, n_ctx)
                xc = _out_proj(_out_odd_kernel, "out_odd", (hf_c, hb_c, so_c, na_c, g_c), xc, mod_c, wout, (ng,), lng, lnb, tm_ctx)
            xl = xl_new
    return xl
```

```python
import functools
import math

import numpy as np
import jax
import jax.numpy as jnp
from jax import lax
from jax.experimental import pallas as pl
from jax.experimental.pallas import tpu as pltpu

F32 = jnp.float32
BF16 = jnp.bfloat16

D_MODEL = 1024
DEPTH = 4
GRID_W = 64
ROPE_BASE = 10000.0
LN_EPS = 1e-5
RMS_EPS = 1e-6
DEEPNORM_ALPHA = (2.0 * DEPTH) ** 0.25

DA_HEADS, DA_QK, DA_V = 4, 64, 128
LOG2E = math.log2(math.e)
DA_SCALE = DA_QK ** -0.5 * LOG2E
MLA_HEADS, MLA_Q_RANK, MLA_KV_RANK, MLA_NOPE, MLA_ROPE, MLA_V = 8, 256, 128, 64, 32, 64
MLA_SCALE = (MLA_NOPE + MLA_ROPE) ** -0.5 * LOG2E
ML_HEADS, ML_DH, ML_CONV, ML_CHUNK = 4, 128, 5, 128
ML_WIDTH = ML_HEADS * ML_DH
NA_HEADS, NA_DH, NA_ROWS, NA_COLS = 8, 64, 8, 16
NA_WIDTH = NA_HEADS * NA_DH
NA_SCALE = NA_DH ** -0.5

LANES = 128
MASK_VALUE = -1e30
VMEM_LIMIT = 48 * 1024 * 1024

EV_COLS = 3200
EV_V, EV_CQ, EV_CKV, EV_KRA, EV_KRB, EV_GATE = 1024, 1536, 1792, 1920, 2048, 2176
OD_COLS = 4736
OD_V, OD_O, OD_GATES, OD_QN, OD_KN, OD_VN, OD_GATE = 1024, 1536, 2048, 2176, 2688, 3200, 3712
ONES_ROWS = 16
VA_ROWS = DA_V + ONES_ROWS
VM_ROWS = MLA_V + ONES_ROWS
VT_ROWS = DA_HEADS * VA_ROWS + MLA_HEADS * VM_ROWS
QT_ROWS = 1280
N_UNITS = 16


def _cparams(sem):
    return pltpu.CompilerParams(dimension_semantics=sem, vmem_limit_bytes=VMEM_LIMIT)


def _silu(x):
    return x * jax.nn.sigmoid(x)


def _dot(a, b):
    return jnp.dot(a, b, preferred_element_type=F32)


def _log_sigmoid(x):
    return jnp.minimum(x, 0.0) - jnp.log(1.0 + jnp.exp(-jnp.abs(x)))


def _ada_kernel(c_ref, w_ref, b_ref, o_ref):
    c = c_ref[...]
    o_ref[0] = _dot(_silu(c).astype(BF16), w_ref[0].astype(BF16)) + b_ref[0]


def _ada_mod(cond8, ada_w, ada_b):
    depth, d, e = ada_w.shape
    tn = 1024
    return pl.pallas_call(
        _ada_kernel,
        grid=(depth, e // tn),
        in_specs=[pl.BlockSpec((8, d), lambda l, n: (0, 0)),
                  pl.BlockSpec((1, d, tn), lambda l, n: (l, 0, n)),
                  pl.BlockSpec((1, 1, tn), lambda l, n: (l, 0, n))],
        out_specs=pl.BlockSpec((1, 8, tn), lambda l, n: (l, 0, n)),
        out_shape=jax.ShapeDtypeStruct((depth, 8, e), F32),
        compiler_params=_cparams(("parallel", "parallel")),
        name="ada_mod",
    )(cond8, ada_w, ada_b.reshape(depth, 1, e))


def _mod_spec(mod):
    per_batch = mod.shape[0] > 1
    return pl.BlockSpec((1, 1, mod.shape[2]), (lambda b, i: (b, 0, 0)) if per_batch else (lambda b, i: (0, 0, 0)))


def _proj_even_kernel(x_ref, mod_ref, rope_ref, w_ref, b_ref, qg_ref, kvg_ref, wuq_ref, wukv_ref,
                      qt_ref, ka_ref, km_ref, vt_ref, g_ref):
    d = D_MODEL
    x = x_ref[0]
    mod = mod_ref[0]
    h = (x * (1.0 + mod[:, d:2 * d]) + mod[:, 0:d]).astype(BF16)
    z = _dot(h, w_ref[...]) + b_ref[...]
    rope = rope_ref[...]
    cos_a, sin_a, cos_m, sin_m, sin_ms = (rope[:, LANES * i:LANES * (i + 1)] for i in range(5))

    def blk(o):
        return z[:, o:o + LANES]

    def rot(a, b, c, s):
        return a * c - b * s, b * c + a * s

    q1a, q1b = rot(blk(0), blk(128), cos_a, sin_a)
    q2a, q2b = rot(blk(256), blk(384), cos_a, sin_a)
    k1a, k1b = rot(blk(512), blk(640), cos_a, sin_a)
    k2a, k2b = rot(blk(768), blk(896), cos_a, sin_a)
    for i, t in enumerate((q1a, q1b, q2a, q2b)):
        qt_ref[0, LANES * i:LANES * (i + 1), :] = (t * DA_SCALE).T.astype(BF16)
    for i, t in enumerate((k1a, k1b, k2a, k2b)):
        ka_ref[0, i // 2, :, LANES * (i % 2):LANES * (i % 2 + 1)] = t.astype(BF16)
    for i in range(4):
        vt_ref[0, VA_ROWS * i:VA_ROWS * i + DA_V, :] = blk(EV_V + LANES * i).T.astype(BF16)
        vt_ref[0, VA_ROWS * i + DA_V:VA_ROWS * (i + 1), :] = jnp.ones((ONES_ROWS, x.shape[0]), BF16)

    def rms(t, g):
        return (t * lax.rsqrt(jnp.mean(t * t, axis=-1, keepdims=True) + RMS_EPS) * g).astype(BF16)

    qm = _dot(rms(z[:, EV_CQ:EV_CQ + MLA_Q_RANK], qg_ref[...]), wuq_ref[...])
    qra, qrb = rot(qm[:, 512:640], qm[:, 640:768], cos_m, sin_m)
    for i in range(4):
        qt_ref[0, 512 + LANES * i:512 + LANES * (i + 1), :] = (qm[:, LANES * i:LANES * (i + 1)] * MLA_SCALE).T.astype(BF16)
    qt_ref[0, 1024:1152, :] = (qra * MLA_SCALE).T.astype(BF16)
    qt_ref[0, 1152:1280, :] = (qrb * MLA_SCALE).T.astype(BF16)

    kvm = _dot(rms(z[:, EV_CKV:EV_CKV + MLA_KV_RANK], kvg_ref[...]), wukv_ref[...])
    k_rope = (blk(EV_KRA) * cos_m + blk(EV_KRB) * sin_ms).astype(BF16)
    for g in range(4):
        km_ref[0, g, :, 0:LANES] = kvm[:, LANES * g:LANES * (g + 1)].astype(BF16)
        km_ref[0, g, :, LANES:2 * LANES] = k_rope
        vm_t = kvm[:, 512 + LANES * g:512 + LANES * (g + 1)].T.astype(BF16)
        for half in range(2):
            r0 = DA_HEADS * VA_ROWS + VM_ROWS * (2 * g + half)
            vt_ref[0, r0:r0 + MLA_V, :] = vm_t[MLA_V * half:MLA_V * (half + 1), :]
            vt_ref[0, r0 + MLA_V:r0 + VM_ROWS, :] = jnp.ones((ONES_ROWS, x.shape[0]), BF16)

    g_ref[0] = _silu(z[:, EV_GATE:EV_GATE + d]).astype(BF16)


def _proj_even(x, mod, rope, w, b, qg, kvg, wuq, wukv, tm):
    bsz, n, d = x.shape
    const = lambda shape: pl.BlockSpec(shape, lambda b_, i: (0,) * len(shape))
    return pl.pallas_call(
        _proj_even_kernel,
        grid=(bsz, n // tm),
        in_specs=[pl.BlockSpec((1, tm, d), lambda b_, i: (b_, i, 0)),
                  _mod_spec(mod),
                  pl.BlockSpec((tm, 5 * LANES), lambda b_, i: (i, 0)),
                  const((d, EV_COLS)), const((1, EV_COLS)), const((1, MLA_Q_RANK)), const((1, MLA_KV_RANK)),
                  const((MLA_Q_RANK, 768)), const((MLA_KV_RANK, 1024))],
        out_specs=[pl.BlockSpec((1, QT_ROWS, tm), lambda b_, i: (b_, 0, i)),
                   pl.BlockSpec((1, 2, tm, 256), lambda b_, i: (b_, 0, i, 0)),
                   pl.BlockSpec((1, 4, tm, 256), lambda b_, i: (b_, 0, i, 0)),
                   pl.BlockSpec((1, VT_ROWS, tm), lambda b_, i: (b_, 0, i)),
                   pl.BlockSpec((1, tm, d), lambda b_, i: (b_, i, 0))],
        out_shape=[jax.ShapeDtypeStruct((bsz, QT_ROWS, n), BF16),
                   jax.ShapeDtypeStruct((bsz, 2, n, 256), BF16),
                   jax.ShapeDtypeStruct((bsz, 4, n, 256), BF16),
                   jax.ShapeDtypeStruct((bsz, VT_ROWS, n), BF16),
                   jax.ShapeDtypeStruct((bsz, n, d), BF16)],
        compiler_params=_cparams(("parallel", "parallel")),
        name="proj_even",
    )(x, mod, rope, w, b, qg, kvg, wuq, wukv)


def _attn_even_kernel(*refs, tq, nkv, sub_tk, has_ctx, lam_init):
    if has_ctx:
        (lam_ref, subg_ref, qt_ref, ka_ref, km_ref, vt_ref, kac_ref, kmc_ref, vtc_ref,
         y_ref, qw, acc_a, acc_m, m_s) = refs
    else:
        (lam_ref, subg_ref, qt_ref, ka_ref, km_ref, vt_ref,
         y_ref, qw, acc_a, acc_m, m_s) = refs
    j = pl.program_id(2)

    def process(ka, km, vt, n_sub, sub):
        def scores(t, u):
            rows = slice(sub * t, sub * (t + 1))
            k = ka[0, u // DA_HEADS, rows, :] if u < 2 * DA_HEADS else km[0, (u - 2 * DA_HEADS) // 2, rows, :]
            return _dot(k, qw[u])

        def softmax_step(u, s):
            m_old = m_s[u]
            m_new = jnp.maximum(m_old, jnp.max(s, axis=0, keepdims=True))
            alpha = jnp.exp2(m_old - m_new)
            p = jnp.exp2(s - m_new)
            m_s[u] = m_new
            return alpha, p.astype(BF16)

        def pv(t, u, alpha, p):
            cols = slice(sub * t, sub * (t + 1))
            if u < 2 * DA_HEADS:
                hd = u % DA_HEADS
                acc_a[u] = alpha * acc_a[u] + _dot(vt[0, VA_ROWS * hd:VA_ROWS * (hd + 1), cols], p)
            else:
                hd = u - 2 * DA_HEADS
                v = vt[0, DA_HEADS * VA_ROWS + VM_ROWS * hd:DA_HEADS * VA_ROWS + VM_ROWS * (hd + 1), cols]
                acc_m[hd] = alpha * acc_m[hd] + _dot(v, p)

        pairs = [(t, u) for t in range(n_sub) for u in range(N_UNITS)]
        s_next = scores(*pairs[0])
        pending = None
        for i, (t, u) in enumerate(pairs):
            s_cur = s_next
            if i + 1 < len(pairs):
                s_next = scores(*pairs[i + 1])
            if pending is not None:
                pv(*pending)
            alpha, p = softmax_step(u, s_cur)
            pending = (t, u, alpha, p)
        pv(*pending)

    @pl.when(j == 0)
    def _init():
        row = lax.broadcasted_iota(jnp.int32, (256, tq), 0)
        head_of_row = (row >> 5) & (DA_HEADS - 1)
        for mp in range(2):
            q = qt_ref[0, 256 * mp:256 * (mp + 1), :]
            for hd in range(DA_HEADS):
                qw[mp * DA_HEADS + hd] = jnp.where(head_of_row == hd, q, jnp.zeros_like(q))
        for hd in range(MLA_HEADS):
            u = 2 * DA_HEADS + hd
            qw[u] = jnp.zeros((256, tq), BF16)
            qw[u, 64 * (hd % 2):64 * (hd % 2) + 64, :] = qt_ref[0, 512 + 64 * hd:512 + 64 * hd + 64, :]
            qw[u, 128:144, :] = qt_ref[0, 1024 + 16 * hd:1024 + 16 * hd + 16, :]
            qw[u, 144:160, :] = qt_ref[0, 1152 + 16 * hd:1152 + 16 * hd + 16, :]
        acc_a[...] = jnp.zeros(acc_a.shape, F32)
        acc_m[...] = jnp.zeros(acc_m.shape, F32)
        m_s[...] = jnp.full(m_s.shape, MASK_VALUE, F32)
        if has_ctx:
            process(kac_ref, kmc_ref, vtc_ref, 1, kac_ref.shape[2])

    process(ka_ref, km_ref, vt_ref, ka_ref.shape[2] // sub_tk, sub_tk)

    @pl.when(j == nkv - 1)
    def _finish():
        lm = lam_ref[...]
        lam = (jnp.exp(jnp.sum(lm[0:1] * lm[1:2], axis=-1, keepdims=True))
               - jnp.exp(jnp.sum(lm[2:3] * lm[3:4], axis=-1, keepdims=True)) + lam_init)
        for hd in range(DA_HEADS):
            a1, a2 = acc_a[hd], acc_a[DA_HEADS + hd]
            o = a1[0:DA_V] / a1[DA_V:DA_V + 1] - lam * (a2[0:DA_V] / a2[DA_V:DA_V + 1])
            o = o * lax.rsqrt(jnp.mean(o * o, axis=0, keepdims=True) + RMS_EPS) * subg_ref[...] * (1.0 - lam_init)
            y_ref[0, :, DA_V * hd:DA_V * (hd + 1)] = o.T.astype(BF16)
        for g in range(MLA_HEADS // 2):
            b1, b2 = acc_m[2 * g], acc_m[2 * g + 1]
            o = jnp.concatenate([b1[0:MLA_V] / b1[MLA_V:MLA_V + 1], b2[0:MLA_V] / b2[MLA_V:MLA_V + 1]], axis=0)
            y_ref[0, :, 512 + LANES * g:512 + LANES * (g + 1)] = o.T.astype(BF16)


def _attn_even(lam_p, subg, qt, ka, km, vt, ctx_kv, lam_init, tq, tk):
    bsz, _, n = qt.shape
    nk = ka.shape[2]
    nq, nkv = n // tq, nk // tk
    has_ctx = ctx_kv is not None
    in_specs = [pl.BlockSpec((4, DA_QK), lambda b, i, j: (0, 0)),
                pl.BlockSpec((DA_V, 1), lambda b, i, j: (0, 0)),
                pl.BlockSpec((1, QT_ROWS, tq), lambda b, i, j: (b, 0, i)),
                pl.BlockSpec((1, 2, tk, 256), lambda b, i, j: (b, 0, j, 0)),
                pl.BlockSpec((1, 4, tk, 256), lambda b, i, j: (b, 0, j, 0)),
                pl.BlockSpec((1, VT_ROWS, tk), lambda b, i, j: (b, 0, j))]
    args = [lam_p, subg, qt, ka, km, vt]
    if has_ctx:
        tc = ctx_kv[0].shape[2]
        in_specs += [pl.BlockSpec((1, 2, tc, 256), lambda b, i, j: (b, 0, 0, 0)),
                     pl.BlockSpec((1, 4, tc, 256), lambda b, i, j: (b, 0, 0, 0)),
                     pl.BlockSpec((1, VT_ROWS, tc), lambda b, i, j: (b, 0, 0))]
        args += list(ctx_kv)
    return pl.pallas_call(
        functools.partial(_attn_even_kernel, tq=tq, nkv=nkv, sub_tk=min(tk, 512), has_ctx=has_ctx, lam_init=lam_init),
        grid=(bsz, nq, nkv),
        in_specs=in_specs,
        out_specs=pl.BlockSpec((1, tq, D_MODEL), lambda b, i, j: (b, i, 0)),
        out_shape=jax.ShapeDtypeStruct((bsz, n, D_MODEL), BF16),
        scratch_shapes=[pltpu.VMEM((N_UNITS, 256, tq), BF16),
                        pltpu.VMEM((2 * DA_HEADS, VA_ROWS, tq), F32),
                        pltpu.VMEM((MLA_HEADS, VM_ROWS, tq), F32),
                        pltpu.VMEM((N_UNITS, 1, tq), F32)],
        compiler_params=_cparams(("parallel", "parallel", "arbitrary")),
        name="attn_even",
    )(*args)


def _layer_norm_rows(t, g, b):
    mu = jnp.mean(t, axis=-1, keepdims=True)
    tc = t - mu
    var = jnp.mean(tc * tc, axis=-1, keepdims=True)
    return tc * lax.rsqrt(var + LN_EPS) * g + b


def _out_even_kernel(y_ref, g_ref, x_ref, mod_ref, w_ref, lng_ref, lnb_ref, o_ref):
    d = D_MODEL
    y = (y_ref[0].astype(F32) * g_ref[0].astype(F32)).astype(BF16)
    out = _dot(y, w_ref[...])
    t = DEEPNORM_ALPHA * x_ref[0] + mod_ref[0][:, 2 * d:3 * d] * out
    o_ref[0] = _layer_norm_rows(t, lng_ref[...], lnb_ref[...])


def _out_odd_kernel(hf_ref, hb_ref, so_ref, na_ref, g_ref, x_ref, mod_ref, w_ref, ng_ref, lng_ref, lnb_ref, o_ref):
    d = D_MODEL
    hsum = hf_ref[0] + hb_ref[0]
    parts = []
    for hd in range(ML_HEADS):
        t = hsum[:, ML_DH * hd:ML_DH * (hd + 1)]
        mu = jnp.mean(t, axis=-1, keepdims=True)
        tc = t - mu
        var = jnp.mean(tc * tc, axis=-1, keepdims=True)
        parts.append(tc * lax.rsqrt(var + LN_EPS))
    y_ml = jnp.concatenate(parts, axis=-1) * ng_ref[...] * so_ref[0].astype(F32)
    g = g_ref[0].astype(F32)
    y = jnp.concatenate([y_ml * g[:, 0:ML_WIDTH], na_ref[0].astype(F32) * g[:, ML_WIDTH:]], axis=-1).astype(BF16)
    out = _dot(y, w_ref[...])
    t = DEEPNORM_ALPHA * x_ref[0] + mod_ref[0][:, 2 * d:3 * d] * out
    o_ref[0] = _layer_norm_rows(t, lng_ref[...], lnb_ref[...])


def _out_proj(kernel_fn, name, acts, x, mod, w, extra, ln_g, ln_b, tm):
    bsz, n, d = x.shape
    tok = lambda a: pl.BlockSpec((1, tm, a.shape[2]), lambda b_, i: (b_, i, 0))
    const = lambda a: pl.BlockSpec(a.shape, lambda b_, i: (0,) * a.ndim)
    ins = list(acts) + [x, mod, w] + list(extra) + [ln_g, ln_b]
    specs = [tok(a) for a in acts] + [tok(x), _mod_spec(mod), const(w)] + [const(e) for e in extra] + [const(ln_g), const(ln_b)]
    return pl.pallas_call(
        kernel_fn,
        grid=(bsz, n // tm),
        in_specs=specs,
        out_specs=pl.BlockSpec((1, tm, d), lambda b_, i: (b_, i, 0)),
        out_shape=jax.ShapeDtypeStruct((bsz, n, d), F32),
        compiler_params=_cparams(("parallel", "parallel")),
        name=name,
    )(*ins)


def _proj_odd_kernel(x_ref, mod_ref, w_ref, b_ref, qk_ref, v_ref, so_ref, gt_ref, qn_ref, kn_ref, vn_ref, g_ref):
    d = D_MODEL
    x = x_ref[0]
    mod = mod_ref[0]
    h = (x * (1.0 + mod[:, d:2 * d]) + mod[:, 0:d]).astype(BF16)
    z = _dot(h, w_ref[...]) + b_ref[...]
    qk_ref[0] = z[:, 0:2 * ML_WIDTH]
    v_ref[0] = z[:, OD_V:OD_V + ML_WIDTH].astype(BF16)
    so_ref[0] = jax.nn.sigmoid(z[:, OD_O:OD_O + ML_WIDTH]).astype(BF16)
    gt_ref[0] = z[:, OD_GATES:OD_GATES + LANES]
    qn_ref[0] = (z[:, OD_QN:OD_QN + NA_WIDTH] * NA_SCALE).astype(BF16)
    kn_ref[0] = z[:, OD_KN:OD_KN + NA_WIDTH].astype(BF16)
    vn_ref[0] = z[:, OD_VN:OD_VN + NA_WIDTH].astype(BF16)
    g_ref[0] = _silu(z[:, OD_GATE:OD_GATE + d]).astype(BF16)


def _proj_odd(x, mod, w, b, tm):
    bsz, n, d = x.shape
    const = lambda shape: pl.BlockSpec(shape, lambda b_, i: (0,) * len(shape))
    widths = [(2 * ML_WIDTH, F32), (ML_WIDTH, BF16), (ML_WIDTH, BF16), (LANES, F32),
              (NA_WIDTH, BF16), (NA_WIDTH, BF16), (NA_WIDTH, BF16), (d, BF16)]
    return pl.pallas_call(
        _proj_odd_kernel,
        grid=(bsz, n // tm),
        in_specs=[pl.BlockSpec((1, tm, d), lambda b_, i: (b_, i, 0)), _mod_spec(mod),
                  const((d, OD_COLS)), const((1, OD_COLS))],
        out_specs=[pl.BlockSpec((1, tm, wd), lambda b_, i: (b_, i, 0)) for wd, _ in widths],
        out_shape=[jax.ShapeDtypeStruct((bsz, n, wd), dt) for wd, dt in widths],
        compiler_params=_cparams(("parallel", "parallel")),
        name="proj_odd",
    )(x, mod, w, b)


def _conv_kernel(cur_ref, prev_ref, next_ref, w_ref, b_ref, q_ref, kt_ref, buf, *, tm, nt):
    i = pl.program_id(1)
    buf[0:8, :] = jnp.where(i > 0, prev_ref[0], 0.0)
    buf[8:8 + tm, :] = cur_ref[0]
    buf[8 + tm:16 + tm, :] = jnp.where(i < nt - 1, next_ref[0], 0.0)
    w = w_ref[...]
    acc = b_ref[...] + buf[6:6 + tm, :] * w[0:1]
    for tap in range(1, ML_CONV):
        acc = acc + buf[6 + tap:6 + tap + tm, :] * w[tap:tap + 1]
    y = _silu(acc)
    q_ref[0] = y[:, 0:ML_WIDTH].astype(BF16)
    for hd in range(ML_HEADS):
        k = y[:, ML_WIDTH + ML_DH * hd:ML_WIDTH + ML_DH * (hd + 1)] * (ML_DH ** -0.5)
        kt_ref[0, ML_DH * hd:ML_DH * (hd + 1), :] = k.T.astype(BF16)


def _conv_silu(qk, w, b, tm):
    bsz, n, c = qk.shape
    nt = n // tm
    r = tm // 8
    last8 = n // 8 - 1
    return pl.pallas_call(
        functools.partial(_conv_kernel, tm=tm, nt=nt),
        grid=(bsz, nt),
        in_specs=[pl.BlockSpec((1, tm, c), lambda b_, i: (b_, i, 0)),
                  pl.BlockSpec((1, 8, c), lambda b_, i: (b_, jnp.maximum(i * r - 1, 0), 0)),
                  pl.BlockSpec((1, 8, c), lambda b_, i: (b_, jnp.minimum((i + 1) * r, last8), 0)),
                  pl.BlockSpec((ML_CONV, c), lambda b_, i: (0, 0)),
                  pl.BlockSpec((1, c), lambda b_, i: (0, 0))],
        out_specs=[pl.BlockSpec((1, tm, ML_WIDTH), lambda b_, i: (b_, i, 0)),
                   pl.BlockSpec((1, ML_WIDTH, tm), lambda b_, i: (b_, 0, i))],
        out_shape=[jax.ShapeDtypeStruct((bsz, n, ML_WIDTH), BF16),
                   jax.ShapeDtypeStruct((bsz, ML_WIDTH, n), BF16)],
        scratch_shapes=[pltpu.VMEM((tm + 16, c), F32)],
        compiler_params=_cparams(("parallel", "parallel")),
        name="conv_silu",
    )(qk, qk, qk, w, b)


def _scan_rows(x, reverse, op, fill):
    n = x.shape[0]
    row = lax.broadcasted_iota(jnp.int32, x.shape, 0)
    sh = 1
    while sh < n:
        if reverse:
            x = op(x, jnp.where(row + sh < n, pltpu.roll(x, n - sh, 0), fill))
        else:
            x = op(x, jnp.where(row >= sh, pltpu.roll(x, sh, 0), fill))
        sh *= 2
    return x


def _mlstm_kernel(qf_ref, ktf_ref, vf_ref, gf_ref, qb_ref, ktb_ref, vb_ref, gb_ref, fb_ref, c0_ref, m0_ref,
                  hf_ref, hb_ref, c1_ref, m1_ref, c_s, m_s, *, nc):
    L, d = ML_CHUNK, ML_DH
    step = pl.program_id(1)

    @pl.when(step == 0)
    def _load_state():
        c_s[...] = c0_ref[0]
        m_s[...] = m0_ref[0]

    lane = lax.broadcasted_iota(jnp.int32, (L, LANES), 1)
    is_forget = (lane & ML_HEADS) != 0
    ti = lax.broadcasted_iota(jnp.int32, (L, L), 0)
    si = lax.broadcasted_iota(jnp.int32, (L, L), 1)
    ones = jnp.ones((L, d), BF16)
    dirs = ((False, qf_ref, ktf_ref, vf_ref, gf_ref, hf_ref), (True, qb_ref, ktb_ref, vb_ref, gb_ref, hb_ref))
    for di, (reverse, q_ref, kt_ref, v_ref, g_ref, h_ref) in enumerate(dirs):
        gates = g_ref[0]
        g2 = jnp.where(is_forget, _log_sigmoid(gates + fb_ref[...]), gates)
        a_cum = _scan_rows(jnp.where(is_forget, g2, 0.0), reverse, jnp.add, 0.0)
        b_t = pltpu.roll(g2, ML_HEADS, 1) - a_cum
        last = 0 if reverse else L - 1
        a_tot = a_cum[last:last + 1, :]
        m_old = m_s[di]
        mm = jnp.maximum(m_old, jnp.max(b_t, axis=0, keepdims=True))
        w_s_t = jnp.exp(b_t - mm).T
        decay = jnp.exp(m_old - mm)
        m_s[di] = a_tot + mm
        big_m = jnp.maximum(m_old, _scan_rows(b_t, reverse, jnp.maximum, MASK_VALUE))
        m_t = a_cum + big_m
        b_rows = b_t.T
        causal = (ti <= si) if reverse else (ti >= si)
        base = 2 * ML_HEADS * di + ML_HEADS
        for hd in range(ML_HEADS):
            f = base + hd
            ch = ML_HEADS * di + hd
            q = q_ref[0, :, d * hd:d * (hd + 1)]
            kt = kt_ref[0, d * hd:d * (hd + 1), :]
            vext = jnp.concatenate([v_ref[0, :, d * hd:d * (hd + 1)], ones], axis=-1)
            c_old = c_s[ch]
            big_m_b = jnp.broadcast_to(big_m[:, f:f + 1], (L, L))
            m_t_b = jnp.broadcast_to(m_t[:, f:f + 1], (L, d))
            m_old_b = jnp.broadcast_to(m_old[:, f:f + 1], (L, d))
            w_ts = jnp.exp(jnp.where(causal, b_rows[f:f + 1, :] - big_m_b, MASK_VALUE)) * _dot(q, kt)
            w_c = jnp.exp(m_old_b - big_m_b)
            lhs = jnp.concatenate([w_ts.astype(BF16), (q.astype(F32) * w_c).astype(BF16)], axis=-1)
            rhs = jnp.concatenate([vext, c_old.astype(BF16)], axis=0)
            out = _dot(lhs, rhs)
            h_ref[0, :, d * hd:d * (hd + 1)] = out[:, 0:d] / jnp.maximum(jnp.abs(out[:, d:]), jnp.exp(-m_t_b))
            kw_t = (kt.astype(F32) * w_s_t[f:f + 1, :]).astype(BF16)
            c_s[ch] = decay[:, f:f + 1] * c_old + _dot(kw_t, vext)

    @pl.when(step == nc - 1)
    def _store_state():
        c1_ref[0] = c_s[...]
        m1_ref[0] = m_s[...]


def _mlstm(q, kt, v, gates, fb_lane, state):
    bsz, n, _ = q.shape
    nc = n // ML_CHUNK
    L, w = ML_CHUNK, ML_WIDTH
    fwd = lambda b_, s: (b_, s, 0)
    bwd = lambda b_, s: (b_, nc - 1 - s, 0)
    fwd_t = lambda b_, s: (b_, 0, s)
    bwd_t = lambda b_, s: (b_, 0, nc - 1 - s)
    st = lambda b_, s: (b_, 0, 0, 0)
    c0, m0 = state
    c_shape, m_shape = (2 * ML_HEADS, ML_DH, 2 * ML_DH), (2, 1, LANES)
    tok = lambda width, im: pl.BlockSpec((1, L, width), im)
    out = pl.pallas_call(
        functools.partial(_mlstm_kernel, nc=nc),
        grid=(bsz, nc),
        in_specs=[tok(w, fwd), pl.BlockSpec((1, w, L), fwd_t), tok(w, fwd), tok(LANES, fwd),
                  tok(w, bwd), pl.BlockSpec((1, w, L), bwd_t), tok(w, bwd), tok(LANES, bwd),
                  pl.BlockSpec((1, LANES), lambda b_, s: (0, 0)),
                  pl.BlockSpec((1,) + c_shape, st), pl.BlockSpec((1,) + m_shape, st)],
        out_specs=[tok(w, fwd), tok(w, bwd), pl.BlockSpec((1,) + c_shape, st), pl.BlockSpec((1,) + m_shape, st)],
        out_shape=[jax.ShapeDtypeStruct((bsz, n, w), F32), jax.ShapeDtypeStruct((bsz, n, w), F32),
                   jax.ShapeDtypeStruct((bsz,) + c_shape, F32), jax.ShapeDtypeStruct((bsz,) + m_shape, F32)],
        scratch_shapes=[pltpu.VMEM(c_shape, F32), pltpu.VMEM(m_shape, F32)],
        compiler_params=_cparams(("parallel", "arbitrary")),
        name="mlstm",
    )(q, kt, v, gates, q, kt, v, gates, fb_lane, c0, m0)
    return out[0], out[1], (out[2], out[3])


def _na_kernel(*refs, n_blocks, rows, qb):
    if n_blocks:
        q_ref, kc_ref, vc_ref, kp_ref, k0_ref, k1_ref, vp_ref, v0_ref, v1_ref, bias_ref, o_ref = refs
        k_lat, v_lat = (kp_ref, k0_ref, k1_ref), (vp_ref, v0_ref, v1_ref)
    else:
        q_ref, kc_ref, vc_ref, o_ref = refs
        k_lat, v_lat = (), ()
    i = pl.program_id(1)
    rpb = qb // GRID_W
    if n_blocks:
        qi = lax.broadcasted_iota(jnp.int32, (qb, n_blocks * qb), 0)
        ki = lax.broadcasted_iota(jnp.int32, (qb, n_blocks * qb), 1)
        q_row = rpb * i + (qi >> 6)
        k_row = rpb * (i - 1) + (ki >> 6)
        start = jnp.clip(q_row - NA_ROWS // 2, 0, rows - NA_ROWS)
        row_ok = (k_row >= start) & (k_row < start + NA_ROWS)
    lane = lax.broadcasted_iota(jnp.int32, (qb, LANES), 1)
    nt = (((1,), (1,)), ((), ()))
    for g in range(NA_HEADS // 2):
        sl = slice(LANES * g, LANES * (g + 1))
        q2 = q_ref[0, :, sl]
        outs = []
        for half in range(2):
            hd = 2 * g + half
            q = jnp.where((lane >> 6) == half, q2, jnp.zeros_like(q2))
            s_c = lax.dot_general(q, kc_ref[0, :, sl], nt, preferred_element_type=F32)
            mx = jnp.max(s_c, axis=-1, keepdims=True)
            if n_blocks:
                s_l = jnp.concatenate([lax.dot_general(q, kr[0, :, sl], nt, preferred_element_type=F32)
                                       for kr in k_lat], axis=-1)
                s_l = jnp.where(row_ok, s_l + bias_ref[hd].astype(F32), MASK_VALUE)
                mx = jnp.maximum(mx, jnp.max(s_l, axis=-1, keepdims=True))
                p_l = jnp.exp(s_l - mx)
            p_c = jnp.exp(s_c - mx)
            den = jnp.sum(p_c, axis=-1, keepdims=True)
            acc = _dot(p_c.astype(BF16), vc_ref[0, :, sl])
            if n_blocks:
                den = den + jnp.sum(p_l, axis=-1, keepdims=True)
                for t, vr in enumerate(v_lat):
                    acc = acc + _dot(p_l[:, qb * t:qb * (t + 1)].astype(BF16), vr[0, :, sl])
            outs.append(acc / den)
        o_ref[0, :, sl] = jnp.where((lane >> 6) == 0, outs[0], outs[1]).astype(BF16)


def _na_attn(qn, kn, vn, kc, vc, bias, rows, qb):
    bsz, n, w = qn.shape
    t = kc.shape[1]
    n_blocks = 0 if bias is None else 3
    nb = n // qb
    tok = pl.BlockSpec((1, qb, w), lambda b_, i: (b_, i, 0))
    ctx = pl.BlockSpec((1, t, w), lambda b_, i: (b_, 0, 0))
    ins, specs = [qn, kc, vc], [tok, ctx, ctx]
    if n_blocks:
        prev = pl.BlockSpec((1, qb, w), lambda b_, i: (b_, jnp.maximum(i - 1, 0), 0))
        nxt = pl.BlockSpec((1, qb, w), lambda b_, i: (b_, jnp.minimum(i + 1, nb - 1), 0))
        ins += [kn, kn, kn, vn, vn, vn, bias]
        specs += [prev, tok, nxt, prev, tok, nxt, pl.BlockSpec(bias.shape, lambda b_, i: (0, 0, 0))]
    return pl.pallas_call(
        functools.partial(_na_kernel, n_blocks=n_blocks, rows=rows, qb=qb),
        grid=(bsz, nb),
        in_specs=specs,
        out_specs=tok,
        out_shape=jax.ShapeDtypeStruct((bsz, n, w), BF16),
        compiler_params=_cparams(("parallel", "parallel")),
        name="na_attn" if n_blocks else "ctx_attn",
    )(*ins)


def _take_cols(w, idx):
    idx = np.asarray(idx)
    cols = jnp.take(w, jnp.asarray(np.maximum(idx, 0)), axis=-1)
    return jnp.where(jnp.asarray(idx >= 0), cols, jnp.zeros_like(cols))


def _even_col_index():
    idx = []
    half = DA_QK // 2
    for o in (0, 256, 512, 768):
        for part in (0, half):
            idx += [o + DA_QK * hd + part + i for hd in range(DA_HEADS) for i in range(half)]
    idx += list(range(1024, 1536)) + list(range(1536, 1792)) + list(range(1792, 1920))
    kr, hr = 1920, MLA_ROPE // 2
    idx += [kr + i for i in range(MLA_ROPE)] + [-1] * (LANES - MLA_ROPE)
    idx += [kr + hr + i for i in range(hr)] + [kr + i for i in range(hr)] + [-1] * (LANES - MLA_ROPE)
    idx += list(range(1952, 2976))
    assert len(idx) == EV_COLS
    return idx


def _uq_col_index():
    per = MLA_NOPE + MLA_ROPE
    hr = MLA_ROPE // 2
    idx = [per * hd + i for hd in range(MLA_HEADS) for i in range(MLA_NOPE)]
    idx += [per * hd + MLA_NOPE + i for hd in range(MLA_HEADS) for i in range(hr)]
    idx += [per * hd + MLA_NOPE + hr + i for hd in range(MLA_HEADS) for i in range(hr)]
    return idx


def _ukv_col_index():
    per = MLA_NOPE + MLA_V
    idx = [per * hd + i for hd in range(MLA_HEADS) for i in range(MLA_NOPE)]
    idx += [per * hd + MLA_NOPE + i for hd in range(MLA_HEADS) for i in range(MLA_V)]
    return idx


def _odd_col_index():
    idx = list(range(0, 2048)) + list(range(2048, 2064)) + [-1] * (LANES - 16) + list(range(2064, 4624))
    assert len(idx) == OD_COLS
    return idx


def _rope_tables(n_lat, n_ctx):
    t = jnp.arange(n_lat)
    row = (t // GRID_W).astype(F32)
    col = (t % GRID_W).astype(F32)

    def cs(dim):
        nf = dim // 4
        freqs = ROPE_BASE ** (-jnp.arange(nf, dtype=F32) / nf)
        ang = jnp.concatenate([row[:, None] * freqs, col[:, None] * freqs], axis=-1)
        return jnp.cos(ang), jnp.sin(ang)

    ca, sa = cs(DA_QK)
    cm, sm = cs(MLA_ROPE)
    lat = jnp.concatenate([jnp.tile(ca, (1, 4)), jnp.tile(sa, (1, 4)), jnp.tile(cm, (1, 8)), jnp.tile(sm, (1, 8)),
                           jnp.tile(jnp.concatenate([-sm, sm], axis=-1), (1, 4))], axis=-1)
    one, zero = jnp.ones((n_ctx, LANES), F32), jnp.zeros((n_ctx, LANES), F32)
    return lat, jnp.concatenate([one, zero, one, zero, zero], axis=-1)


def _na_bias_tiles(rpb, qb):
    heads, n_dr, n_dc = rpb.shape
    rows_per = qb // GRID_W
    qc, kc = np.arange(GRID_W)[:, None], np.arange(GRID_W)[None, :]
    c0 = np.clip(qc - NA_COLS // 2, 0, GRID_W - NA_COLS)
    col_ok = (kc >= c0) & (kc < c0 + NA_COLS)
    onehot = ((kc - qc + NA_COLS - 1)[None] == np.arange(n_dc)[:, None, None]) & col_ok[None]
    tiles = jnp.einsum('hrj,jp->hrp', rpb, jnp.asarray(onehot.reshape(n_dc, -1), F32),
                       precision=lax.Precision.HIGHEST).reshape(heads, n_dr, GRID_W, GRID_W)
    tiles = jnp.where(jnp.asarray(col_ok)[None, None], tiles, MASK_VALUE)
    dr = (np.arange(3 * rows_per)[None, :] - rows_per) - np.arange(rows_per)[:, None]
    dr_ok = np.abs(dr) <= NA_ROWS - 1
    idx = np.clip(dr + NA_ROWS - 1, 0, n_dr - 1).reshape(-1)
    blocks = jnp.take(tiles, jnp.asarray(idx), axis=1).reshape(heads, rows_per, 3 * rows_per, GRID_W, GRID_W)
    blocks = jnp.where(jnp.asarray(dr_ok)[None, :, :, None, None], blocks, MASK_VALUE)
    return blocks.transpose(0, 1, 3, 2, 4).reshape(heads, qb, 3 * qb)


def _tile_plan(n_lat, n_ctx):
    return {
        "tok_lat": 512,
        "tok_odd": 256,
        "tok_ctx": min(256, n_ctx),
        "attn_q": 512,
        "attn_k": 1024 if n_lat % 1024 == 0 else 512,
        "na_q": 4 * GRID_W,
    }


def kernel(x, c, ctx, c_ctx, ada_w, ada_b, ln_g, ln_b, ev_w_in, ev_b_in, da_lambda, da_subln_g, mla_q_norm_g, mla_kv_norm_g, mla_w_uq, mla_w_ukv, ev_w_out, od_w_in, od_b_in, ml_conv_w, ml_conv_b, ml_f_bias, ml_norm_g, na_rpb, od_w_out):
    bsz, n_lat, d = x.shape
    n_ctx = ctx.shape[1]
    rows = n_lat // GRID_W
    assert d == D_MODEL and bsz + 1 <= 8 and n_lat % 512 == 0 and n_ctx % 256 == 0 and rows >= NA_ROWS

    cond8 = jnp.zeros((8, d), F32).at[:bsz].set(c).at[bsz].set(c_ctx)
    mod_all = _ada_mod(cond8, ada_w, ada_b)
    rope_lat, rope_ctx = _rope_tables(n_lat, n_ctx)

    tp = _tile_plan(n_lat, n_ctx)
    tm_lat, tm_ctx, tq, tk, na_qb = tp["tok_lat"], tp["tok_ctx"], tp["attn_q"], tp["attn_k"], tp["na_q"]
    xl, xc = x, ctx
    for l in range(DEPTH):
        update_ctx = l < DEPTH - 1
        i = l // 2
        mod_l = mod_all[l, :bsz].reshape(bsz, 1, 3 * d)
        mod_c = mod_all[l, bsz].reshape(1, 1, 3 * d)
        lng, lnb = ln_g[l].reshape(1, d), ln_b[l].reshape(1, d)
        if l % 2 == 0:
            lam_init = 0.8 - 0.6 * math.exp(-0.3 * l)
            w = _take_cols(ev_w_in[i], _even_col_index()).astype(BF16)
            b = _take_cols(ev_b_in[i], _even_col_index()).reshape(1, EV_COLS)
            wuq = _take_cols(mla_w_uq[i], _uq_col_index()).astype(BF16)
            wukv = _take_cols(mla_w_ukv[i], _ukv_col_index()).astype(BF16)
            qg = mla_q_norm_g[i].reshape(1, MLA_Q_RANK)
            kvg = mla_kv_norm_g[i].reshape(1, MLA_KV_RANK)
            subg = da_subln_g[i].reshape(DA_V, 1)
            wout = ev_w_out[i].astype(BF16)
            qt_l, ka_l, km_l, vt_l, g_l = _proj_even(xl, mod_l, rope_lat, w, b, qg, kvg, wuq, wukv, tm_lat)
            qt_c, ka_c, km_c, vt_c, g_c = _proj_even(xc, mod_c, rope_ctx, w, b, qg, kvg, wuq, wukv, tm_ctx)
            y_l = _attn_even(da_lambda[i], subg, qt_l, ka_l, km_l, vt_l, (ka_c, km_c, vt_c), lam_init, tq, tk)
            xl_new = _out_proj(_out_even_kernel, "out_even", (y_l, g_l), xl, mod_l, wout, (), lng, lnb, tm_lat)
            if update_ctx:
                y_c = _attn_even(da_lambda[i], subg, qt_c, ka_c, km_c, vt_c, None, lam_init, n_ctx, n_ctx)
                xc = _out_proj(_out_even_kernel, "out_even", (y_c, g_c), xc, mod_c, wout, (), lng, lnb, tm_ctx)
            xl = xl_new
        else:
            w = _take_cols(od_w_in[i], _odd_col_index()).astype(BF16)
            b = _take_cols(od_b_in[i], _odd_col_index()).reshape(1, OD_COLS)
            wout = od_w_out[i].astype(BF16)
            cw, cb = ml_conv_w[i], ml_conv_b[i].reshape(1, 2 * ML_WIDTH)
            fb = ml_f_bias[i].astype(F32)
            fb_lane = jnp.zeros((1, LANES), F32).at[0, 4:8].set(fb[0]).at[0, 12:16].set(fb[1])
            ng = ml_norm_g[i].reshape(1, ML_WIDTH)
            qk_l, v_l, so_l, gt_l, qn_l, kn_l, vn_l, g_l = _proj_odd(xl, mod_l, w, b, tp["tok_odd"])
            qk_c, v_c, so_c, gt_c, qn_c, kn_c, vn_c, g_c = _proj_odd(xc, mod_c, w, b, tm_ctx)
            q_l, kt_l = _conv_silu(qk_l, cw, cb, tp["tok_lat"])
            q_c, kt_c = _conv_silu(qk_c, cw, cb, n_ctx)
            zero = (jnp.zeros((bsz, 2 * ML_HEADS, ML_DH, 2 * ML_DH), F32), jnp.zeros((bsz, 2, 1, LANES), F32))
            hf_c, hb_c, st_c = _mlstm(q_c, kt_c, v_c, gt_c, fb_lane, zero)
            hf_l, hb_l, _ = _mlstm(q_l, kt_l, v_l, gt_l, fb_lane, st_c)
            bias = _na_bias_tiles(na_rpb[i].astype(F32), na_qb)
            na_l = _na_attn(qn_l, kn_l, vn_l, kn_c, vn_c, bias, rows, na_qb)
            xl_new = _out_proj(_out_odd_kernel, "out_odd", (hf_l, hb_l, so_l, na_l, g_l), xl, mod_l, wout, (ng,), lng, lnb, tm_lat)
            if update_ctx:
                na_c = _na_attn(qn_c, None, None, kn_c, vn_c, None, rows, n_ctx)
                xc = _out_proj(_out_odd_kernel, "out_odd", (hf_c, hb_c, so_c, na_c, g_c), xc, mod_c, wout, (ng,), lng, lnb, tm_ctx)
            xl = xl_new
    return xl
```

```python
import functools
import math

import numpy as np
import jax
import jax.numpy as jnp
from jax import lax
from jax.experimental import pallas as pl
from jax.experimental.pallas import tpu as pltpu

F32 = jnp.float32
BF16 = jnp.bfloat16

D_MODEL = 1024
DEPTH = 4
GRID_W = 64
ROPE_BASE = 10000.0
LN_EPS = 1e-5
RMS_EPS = 1e-6
DEEPNORM_ALPHA = (2.0 * DEPTH) ** 0.25

DA_HEADS, DA_QK, DA_V = 4, 64, 128
LOG2E = math.log2(math.e)
DA_SCALE = DA_QK ** -0.5 * LOG2E
MLA_HEADS, MLA_Q_RANK, MLA_KV_RANK, MLA_NOPE, MLA_ROPE, MLA_V = 8, 256, 128, 64, 32, 64
MLA_SCALE = (MLA_NOPE + MLA_ROPE) ** -0.5 * LOG2E
ML_HEADS, ML_DH, ML_CONV, ML_CHUNK = 4, 128, 5, 128
ML_WIDTH = ML_HEADS * ML_DH
NA_HEADS, NA_DH, NA_ROWS, NA_COLS = 8, 64, 8, 16
NA_WIDTH = NA_HEADS * NA_DH
NA_SCALE = NA_DH ** -0.5 * LOG2E

LANES = 128
MASK_VALUE = -1e30
VMEM_LIMIT = 48 * 1024 * 1024

EV_COLS = 3200
EV_V, EV_CQ, EV_CKV, EV_KRA, EV_KRB, EV_GATE = 1024, 1536, 1792, 1920, 2048, 2176
OD_COLS = 4736
OD_V, OD_O, OD_GATES, OD_QN, OD_KN, OD_VN, OD_GATE = 1024, 1536, 2048, 2176, 2688, 3200, 3712
ONES_ROWS = 16
VA_ROWS = DA_V + ONES_ROWS
VM_ROWS = MLA_V + ONES_ROWS
VT_ROWS = DA_HEADS * VA_ROWS + MLA_HEADS * VM_ROWS
QT_ROWS = 1280
N_UNITS = 16
VN_ROWS = LANES + ONES_ROWS


def _cparams(sem):
    return pltpu.CompilerParams(dimension_semantics=sem, vmem_limit_bytes=VMEM_LIMIT)


def _silu(x):
    return x * jax.nn.sigmoid(x)


def _dot(a, b):
    return jnp.dot(a, b, preferred_element_type=F32)


def _log_sigmoid(x):
    return jnp.minimum(x, 0.0) - jnp.log(1.0 + jnp.exp(-jnp.abs(x)))


def _ada_kernel(c_ref, w_ref, b_ref, o_ref):
    c = c_ref[...]
    o_ref[0] = _dot(_silu(c).astype(BF16), w_ref[0].astype(BF16)) + b_ref[0]


def _ada_mod(cond8, ada_w, ada_b):
    depth, d, e = ada_w.shape
    tn = 1024
    return pl.pallas_call(
        _ada_kernel,
        grid=(depth, e // tn),
        in_specs=[pl.BlockSpec((8, d), lambda l, n: (0, 0)),
                  pl.BlockSpec((1, d, tn), lambda l, n: (l, 0, n)),
                  pl.BlockSpec((1, 1, tn), lambda l, n: (l, 0, n))],
        out_specs=pl.BlockSpec((1, 8, tn), lambda l, n: (l, 0, n)),
        out_shape=jax.ShapeDtypeStruct((depth, 8, e), F32),
        compiler_params=_cparams(("parallel", "parallel")),
        name="ada_mod",
    )(cond8, ada_w, ada_b.reshape(depth, 1, e))


def _mod_spec(mod):
    per_batch = mod.shape[0] > 1
    return pl.BlockSpec((1, 1, mod.shape[2]), (lambda b, i: (b, 0, 0)) if per_batch else (lambda b, i: (0, 0, 0)))


def _proj_even_kernel(x_ref, mod_ref, rope_ref, w_ref, b_ref, qg_ref, kvg_ref, wuq_ref, wukv_ref,
                      qt_ref, ka_ref, km_ref, vt_ref, g_ref):
    d = D_MODEL
    x = x_ref[0]
    mod = mod_ref[0]
    h = (x * (1.0 + mod[:, d:2 * d]) + mod[:, 0:d]).astype(BF16)
    z = _dot(h, w_ref[...]) + b_ref[...]
    rope = rope_ref[...]
    cos_a, sin_a, cos_m, sin_m, sin_ms = (rope[:, LANES * i:LANES * (i + 1)] for i in range(5))

    def blk(o):
        return z[:, o:o + LANES]

    def rot(a, b, c, s):
        return a * c - b * s, b * c + a * s

    q1a, q1b = rot(blk(0), blk(128), cos_a, sin_a)
    q2a, q2b = rot(blk(256), blk(384), cos_a, sin_a)
    k1a, k1b = rot(blk(512), blk(640), cos_a, sin_a)
    k2a, k2b = rot(blk(768), blk(896), cos_a, sin_a)
    for i, t in enumerate((q1a, q1b, q2a, q2b)):
        qt_ref[0, LANES * i:LANES * (i + 1), :] = (t * DA_SCALE).T.astype(BF16)
    for i, t in enumerate((k1a, k1b, k2a, k2b)):
        ka_ref[0, i // 2, :, LANES * (i % 2):LANES * (i % 2 + 1)] = t.astype(BF16)
    for i in range(4):
        vt_ref[0, VA_ROWS * i:VA_ROWS * i + DA_V, :] = blk(EV_V + LANES * i).T.astype(BF16)
        vt_ref[0, VA_ROWS * i + DA_V:VA_ROWS * (i + 1), :] = jnp.ones((ONES_ROWS, x.shape[0]), BF16)

    def rms(t, g):
        return (t * lax.rsqrt(jnp.mean(t * t, axis=-1, keepdims=True) + RMS_EPS) * g).astype(BF16)

    qm = _dot(rms(z[:, EV_CQ:EV_CQ + MLA_Q_RANK], qg_ref[...]), wuq_ref[...])
    qra, qrb = rot(qm[:, 512:640], qm[:, 640:768], cos_m, sin_m)
    for i in range(4):
        qt_ref[0, 512 + LANES * i:512 + LANES * (i + 1), :] = (qm[:, LANES * i:LANES * (i + 1)] * MLA_SCALE).T.astype(BF16)
    qt_ref[0, 1024:1152, :] = (qra * MLA_SCALE).T.astype(BF16)
    qt_ref[0, 1152:1280, :] = (qrb * MLA_SCALE).T.astype(BF16)

    kvm = _dot(rms(z[:, EV_CKV:EV_CKV + MLA_KV_RANK], kvg_ref[...]), wukv_ref[...])
    k_rope = (blk(EV_KRA) * cos_m + blk(EV_KRB) * sin_ms).astype(BF16)
    for g in range(4):
        km_ref[0, g, :, 0:LANES] = kvm[:, LANES * g:LANES * (g + 1)].astype(BF16)
        km_ref[0, g, :, LANES:2 * LANES] = k_rope
        vm_t = kvm[:, 512 + LANES * g:512 + LANES * (g + 1)].T.astype(BF16)
        for half in range(2):
            r0 = DA_HEADS * VA_ROWS + VM_ROWS * (2 * g + half)
            vt_ref[0, r0:r0 + MLA_V, :] = vm_t[MLA_V * half:MLA_V * (half + 1), :]
            vt_ref[0, r0 + MLA_V:r0 + VM_ROWS, :] = jnp.ones((ONES_ROWS, x.shape[0]), BF16)

    g_ref[0] = _silu(z[:, EV_GATE:EV_GATE + d]).astype(BF16)


def _proj_even(x, mod, rope, w, b, qg, kvg, wuq, wukv, tm):
    bsz, n, d = x.shape
    const = lambda shape: pl.BlockSpec(shape, lambda b_, i: (0,) * len(shape))
    return pl.pallas_call(
        _proj_even_kernel,
        grid=(bsz, n // tm),
        in_specs=[pl.BlockSpec((1, tm, d), lambda b_, i: (b_, i, 0)),
                  _mod_spec(mod),
                  pl.BlockSpec((tm, 5 * LANES), lambda b_, i: (i, 0)),
                  const((d, EV_COLS)), const((1, EV_COLS)), const((1, MLA_Q_RANK)), const((1, MLA_KV_RANK)),
                  const((MLA_Q_RANK, 768)), const((MLA_KV_RANK, 1024))],
        out_specs=[pl.BlockSpec((1, QT_ROWS, tm), lambda b_, i: (b_, 0, i)),
                   pl.BlockSpec((1, 2, tm, 256), lambda b_, i: (b_, 0, i, 0)),
                   pl.BlockSpec((1, 4, tm, 256), lambda b_, i: (b_, 0, i, 0)),
                   pl.BlockSpec((1, VT_ROWS, tm), lambda b_, i: (b_, 0, i)),
                   pl.BlockSpec((1, tm, d), lambda b_, i: (b_, i, 0))],
        out_shape=[jax.ShapeDtypeStruct((bsz, QT_ROWS, n), BF16),
                   jax.ShapeDtypeStruct((bsz, 2, n, 256), BF16),
                   jax.ShapeDtypeStruct((bsz, 4, n, 256), BF16),
                   jax.ShapeDtypeStruct((bsz, VT_ROWS, n), BF16),
                   jax.ShapeDtypeStruct((bsz, n, d), BF16)],
        compiler_params=_cparams(("parallel", "parallel")),
        name="proj_even",
    )(x, mod, rope, w, b, qg, kvg, wuq, wukv)


def _attn_even_kernel(*refs, tq, nkv, sub_tk, has_ctx, lam_init):
    if has_ctx:
        (lam_ref, subg_ref, qt_ref, ka_ref, km_ref, vt_ref, kac_ref, kmc_ref, vtc_ref,
         y_ref, qw, acc_a, acc_m, m_s) = refs
    else:
        (lam_ref, subg_ref, qt_ref, ka_ref, km_ref, vt_ref,
         y_ref, qw, acc_a, acc_m, m_s) = refs
    j = pl.program_id(2)

    def process(ka, km, vt, n_sub, sub):
        def scores(t, u):
            rows = slice(sub * t, sub * (t + 1))
            k = ka[0, u // DA_HEADS, rows, :] if u < 2 * DA_HEADS else km[0, (u - 2 * DA_HEADS) // 2, rows, :]
            return _dot(k, qw[u])

        def softmax_step(u, s):
            m_old = m_s[u]
            m_new = jnp.maximum(m_old, jnp.max(s, axis=0, keepdims=True))
            alpha = jnp.exp2(m_old - m_new)
            p = jnp.exp2(s - m_new)
            m_s[u] = m_new
            return alpha, p.astype(BF16)

        def pv(t, u, alpha, p):
            cols = slice(sub * t, sub * (t + 1))
            if u < 2 * DA_HEADS:
                hd = u % DA_HEADS
                acc_a[u] = alpha * acc_a[u] + _dot(vt[0, VA_ROWS * hd:VA_ROWS * (hd + 1), cols], p)
            else:
                hd = u - 2 * DA_HEADS
                v = vt[0, DA_HEADS * VA_ROWS + VM_ROWS * hd:DA_HEADS * VA_ROWS + VM_ROWS * (hd + 1), cols]
                acc_m[hd] = alpha * acc_m[hd] + _dot(v, p)

        pairs = [(t, u) for t in range(n_sub) for u in range(N_UNITS)]
        s_next = scores(*pairs[0])
        pending = None
        for i, (t, u) in enumerate(pairs):
            s_cur = s_next
            if i + 1 < len(pairs):
                s_next = scores(*pairs[i + 1])
            if pending is not None:
                pv(*pending)
            alpha, p = softmax_step(u, s_cur)
            pending = (t, u, alpha, p)
        pv(*pending)

    @pl.when(j == 0)
    def _init():
        row = lax.broadcasted_iota(jnp.int32, (256, tq), 0)
        head_of_row = (row >> 5) & (DA_HEADS - 1)
        for mp in range(2):
            q = qt_ref[0, 256 * mp:256 * (mp + 1), :]
            for hd in range(DA_HEADS):
                qw[mp * DA_HEADS + hd] = jnp.where(head_of_row == hd, q, jnp.zeros_like(q))
        for hd in range(MLA_HEADS):
            u = 2 * DA_HEADS + hd
            qw[u] = jnp.zeros((256, tq), BF16)
            qw[u, 64 * (hd % 2):64 * (hd % 2) + 64, :] = qt_ref[0, 512 + 64 * hd:512 + 64 * hd + 64, :]
            qw[u, 128:144, :] = qt_ref[0, 1024 + 16 * hd:1024 + 16 * hd + 16, :]
            qw[u, 144:160, :] = qt_ref[0, 1152 + 16 * hd:1152 + 16 * hd + 16, :]
        acc_a[...] = jnp.zeros(acc_a.shape, F32)
        acc_m[...] = jnp.zeros(acc_m.shape, F32)
        m_s[...] = jnp.full(m_s.shape, MASK_VALUE, F32)
        if has_ctx:
            process(kac_ref, kmc_ref, vtc_ref, 1, kac_ref.shape[2])

    process(ka_ref, km_ref, vt_ref, ka_ref.shape[2] // sub_tk, sub_tk)

    @pl.when(j == nkv - 1)
    def _finish():
        lm = lam_ref[...]
        lam = (jnp.exp(jnp.sum(lm[0:1] * lm[1:2], axis=-1, keepdims=True))
               - jnp.exp(jnp.sum(lm[2:3] * lm[3:4], axis=-1, keepdims=True)) + lam_init)
        for hd in range(DA_HEADS):
            a1, a2 = acc_a[hd], acc_a[DA_HEADS + hd]
            o = a1[0:DA_V] / a1[DA_V:DA_V + 1] - lam * (a2[0:DA_V] / a2[DA_V:DA_V + 1])
            o = o * lax.rsqrt(jnp.mean(o * o, axis=0, keepdims=True) + RMS_EPS) * subg_ref[...] * (1.0 - lam_init)
            y_ref[0, :, DA_V * hd:DA_V * (hd + 1)] = o.T.astype(BF16)
        for g in range(MLA_HEADS // 2):
            b1, b2 = acc_m[2 * g], acc_m[2 * g + 1]
            o = jnp.concatenate([b1[0:MLA_V] / b1[MLA_V:MLA_V + 1], b2[0:MLA_V] / b2[MLA_V:MLA_V + 1]], axis=0)
            y_ref[0, :, 512 + LANES * g:512 + LANES * (g + 1)] = o.T.astype(BF16)


def _attn_even(lam_p, subg, qt, ka, km, vt, ctx_kv, lam_init, tq, tk):
    bsz, _, n = qt.shape
    nk = ka.shape[2]
    nq, nkv = n // tq, nk // tk
    has_ctx = ctx_kv is not None
    in_specs = [pl.BlockSpec((4, DA_QK), lambda b, i, j: (0, 0)),
                pl.BlockSpec((DA_V, 1), lambda b, i, j: (0, 0)),
                pl.BlockSpec((1, QT_ROWS, tq), lambda b, i, j: (b, 0, i)),
                pl.BlockSpec((1, 2, tk, 256), lambda b, i, j: (b, 0, j, 0)),
                pl.BlockSpec((1, 4, tk, 256), lambda b, i, j: (b, 0, j, 0)),
                pl.BlockSpec((1, VT_ROWS, tk), lambda b, i, j: (b, 0, j))]
    args = [lam_p, subg, qt, ka, km, vt]
    if has_ctx:
        tc = ctx_kv[0].shape[2]
        in_specs += [pl.BlockSpec((1, 2, tc, 256), lambda b, i, j: (b, 0, 0, 0)),
                     pl.BlockSpec((1, 4, tc, 256), lambda b, i, j: (b, 0, 0, 0)),
                     pl.BlockSpec((1, VT_ROWS, tc), lambda b, i, j: (b, 0, 0))]
        args += list(ctx_kv)
    return pl.pallas_call(
        functools.partial(_attn_even_kernel, tq=tq, nkv=nkv, sub_tk=min(tk, 512), has_ctx=has_ctx, lam_init=lam_init),
        grid=(bsz, nq, nkv),
        in_specs=in_specs,
        out_specs=pl.BlockSpec((1, tq, D_MODEL), lambda b, i, j: (b, i, 0)),
        out_shape=jax.ShapeDtypeStruct((bsz, n, D_MODEL), BF16),
        scratch_shapes=[pltpu.VMEM((N_UNITS, 256, tq), BF16),
                        pltpu.VMEM((2 * DA_HEADS, VA_ROWS, tq), F32),
                        pltpu.VMEM((MLA_HEADS, VM_ROWS, tq), F32),
                        pltpu.VMEM((N_UNITS, 1, tq), F32)],
        compiler_params=_cparams(("parallel", "parallel", "arbitrary")),
        name="attn_even",
    )(*args)


def _layer_norm_rows(t, g, b):
    mu = jnp.mean(t, axis=-1, keepdims=True)
    tc = t - mu
    var = jnp.mean(tc * tc, axis=-1, keepdims=True)
    return tc * lax.rsqrt(var + LN_EPS) * g + b


def _out_even_kernel(y_ref, g_ref, x_ref, mod_ref, w_ref, lng_ref, lnb_ref, o_ref):
    d = D_MODEL
    y = (y_ref[0].astype(F32) * g_ref[0].astype(F32)).astype(BF16)
    out = _dot(y, w_ref[...])
    t = DEEPNORM_ALPHA * x_ref[0] + mod_ref[0][:, 2 * d:3 * d] * out
    o_ref[0] = _layer_norm_rows(t, lng_ref[...], lnb_ref[...])


def _out_odd_kernel(hf_ref, hb_ref, so_ref, na_ref, g_ref, x_ref, mod_ref, w_ref, ng_ref, lng_ref, lnb_ref, o_ref):
    d = D_MODEL
    hsum = hf_ref[0] + hb_ref[0]
    parts = []
    for hd in range(ML_HEADS):
        t = hsum[:, ML_DH * hd:ML_DH * (hd + 1)]
        mu = jnp.mean(t, axis=-1, keepdims=True)
        tc = t - mu
        var = jnp.mean(tc * tc, axis=-1, keepdims=True)
        parts.append(tc * lax.rsqrt(var + LN_EPS))
    y_ml = jnp.concatenate(parts, axis=-1) * ng_ref[...] * so_ref[0].astype(F32)
    g = g_ref[0].astype(F32)
    y = jnp.concatenate([y_ml * g[:, 0:ML_WIDTH], na_ref[0].astype(F32) * g[:, ML_WIDTH:]], axis=-1).astype(BF16)
    out = _dot(y, w_ref[...])
    t = DEEPNORM_ALPHA * x_ref[0] + mod_ref[0][:, 2 * d:3 * d] * out
    o_ref[0] = _layer_norm_rows(t, lng_ref[...], lnb_ref[...])


def _out_proj(kernel_fn, name, acts, x, mod, w, extra, ln_g, ln_b, tm):
    bsz, n, d = x.shape
    tok = lambda a: pl.BlockSpec((1, tm, a.shape[2]), lambda b_, i: (b_, i, 0))
    const = lambda a: pl.BlockSpec(a.shape, lambda b_, i: (0,) * a.ndim)
    ins = list(acts) + [x, mod, w] + list(extra) + [ln_g, ln_b]
    specs = [tok(a) for a in acts] + [tok(x), _mod_spec(mod), const(w)] + [const(e) for e in extra] + [const(ln_g), const(ln_b)]
    return pl.pallas_call(
        kernel_fn,
        grid=(bsz, n // tm),
        in_specs=specs,
        out_specs=pl.BlockSpec((1, tm, d), lambda b_, i: (b_, i, 0)),
        out_shape=jax.ShapeDtypeStruct((bsz, n, d), F32),
        compiler_params=_cparams(("parallel", "parallel")),
        name=name,
    )(*ins)


def _proj_odd_kernel(x_ref, mod_ref, w_ref, b_ref, qk_ref, v_ref, so_ref, gt_ref, qn_ref, kn_ref, vn_ref, g_ref):
    d = D_MODEL
    x = x_ref[0]
    mod = mod_ref[0]
    h = (x * (1.0 + mod[:, d:2 * d]) + mod[:, 0:d]).astype(BF16)
    z = _dot(h, w_ref[...]) + b_ref[...]
    qk_ref[0] = z[:, 0:2 * ML_WIDTH]
    v_ref[0] = z[:, OD_V:OD_V + ML_WIDTH].astype(BF16)
    so_ref[0] = jax.nn.sigmoid(z[:, OD_O:OD_O + ML_WIDTH]).astype(BF16)
    gt_ref[0] = z[:, OD_GATES:OD_GATES + LANES]
    kn_ref[0] = z[:, OD_KN:OD_KN + NA_WIDTH].astype(BF16)
    for g in range(NA_HEADS // 2):
        qn_ref[0, LANES * g:LANES * (g + 1), :] = (z[:, OD_QN + LANES * g:OD_QN + LANES * (g + 1)] * NA_SCALE).T.astype(BF16)
        vn_ref[0, VN_ROWS * g:VN_ROWS * g + LANES, :] = z[:, OD_VN + LANES * g:OD_VN + LANES * (g + 1)].T.astype(BF16)
        vn_ref[0, VN_ROWS * g + LANES:VN_ROWS * (g + 1), :] = jnp.ones((ONES_ROWS, x.shape[0]), BF16)
    g_ref[0] = _silu(z[:, OD_GATE:OD_GATE + d]).astype(BF16)


def _proj_odd(x, mod, w, b, tm):
    bsz, n, d = x.shape
    const = lambda shape: pl.BlockSpec(shape, lambda b_, i: (0,) * len(shape))
    tok = lambda wd, dt: (pl.BlockSpec((1, tm, wd), lambda b_, i: (b_, i, 0)), jax.ShapeDtypeStruct((bsz, n, wd), dt))
    feat = lambda r, dt: (pl.BlockSpec((1, r, tm), lambda b_, i: (b_, 0, i)), jax.ShapeDtypeStruct((bsz, r, n), dt))
    outs = [tok(2 * ML_WIDTH, F32), tok(ML_WIDTH, BF16), tok(ML_WIDTH, BF16), tok(LANES, F32),
            feat(NA_WIDTH, BF16), tok(NA_WIDTH, BF16), feat(VN_ROWS * NA_HEADS // 2, BF16), tok(d, BF16)]
    return pl.pallas_call(
        _proj_odd_kernel,
        grid=(bsz, n // tm),
        in_specs=[pl.BlockSpec((1, tm, d), lambda b_, i: (b_, i, 0)), _mod_spec(mod),
                  const((d, OD_COLS)), const((1, OD_COLS))],
        out_specs=[o[0] for o in outs],
        out_shape=[o[1] for o in outs],
        compiler_params=_cparams(("parallel", "parallel")),
        name="proj_odd",
    )(x, mod, w, b)


def _conv_kernel(cur_ref, prev_ref, next_ref, w_ref, b_ref, q_ref, kt_ref, buf, *, tm, nt):
    i = pl.program_id(1)
    buf[0:8, :] = jnp.where(i > 0, prev_ref[0], 0.0)
    buf[8:8 + tm, :] = cur_ref[0]
    buf[8 + tm:16 + tm, :] = jnp.where(i < nt - 1, next_ref[0], 0.0)
    w = w_ref[...]
    acc = b_ref[...] + buf[6:6 + tm, :] * w[0:1]
    for tap in range(1, ML_CONV):
        acc = acc + buf[6 + tap:6 + tap + tm, :] * w[tap:tap + 1]
    y = _silu(acc)
    q_ref[0] = y[:, 0:ML_WIDTH].astype(BF16)
    for hd in range(ML_HEADS):
        k = y[:, ML_WIDTH + ML_DH * hd:ML_WIDTH + ML_DH * (hd + 1)] * (ML_DH ** -0.5)
        kt_ref[0, ML_DH * hd:ML_DH * (hd + 1), :] = k.T.astype(BF16)


def _conv_silu(qk, w, b, tm):
    bsz, n, c = qk.shape
    nt = n // tm
    r = tm // 8
    last8 = n // 8 - 1
    return pl.pallas_call(
        functools.partial(_conv_kernel, tm=tm, nt=nt),
        grid=(bsz, nt),
        in_specs=[pl.BlockSpec((1, tm, c), lambda b_, i: (b_, i, 0)),
                  pl.BlockSpec((1, 8, c), lambda b_, i: (b_, jnp.maximum(i * r - 1, 0), 0)),
                  pl.BlockSpec((1, 8, c), lambda b_, i: (b_, jnp.minimum((i + 1) * r, last8), 0)),
                  pl.BlockSpec((ML_CONV, c), lambda b_, i: (0, 0)),
                  pl.BlockSpec((1, c), lambda b_, i: (0, 0))],
        out_specs=[pl.BlockSpec((1, tm, ML_WIDTH), lambda b_, i: (b_, i, 0)),
                   pl.BlockSpec((1, ML_WIDTH, tm), lambda b_, i: (b_, 0, i))],
        out_shape=[jax.ShapeDtypeStruct((bsz, n, ML_WIDTH), BF16),
                   jax.ShapeDtypeStruct((bsz, ML_WIDTH, n), BF16)],
        scratch_shapes=[pltpu.VMEM((tm + 16, c), F32)],
        compiler_params=_cparams(("parallel", "parallel")),
        name="conv_silu",
    )(qk, qk, qk, w, b)


def _scan_rows(x, reverse, op, fill):
    n = x.shape[0]
    row = lax.broadcasted_iota(jnp.int32, x.shape, 0)
    sh = 1
    while sh < n:
        if reverse:
            x = op(x, jnp.where(row + sh < n, pltpu.roll(x, n - sh, 0), fill))
        else:
            x = op(x, jnp.where(row >= sh, pltpu.roll(x, sh, 0), fill))
        sh *= 2
    return x


def _mlstm_kernel(qf_ref, ktf_ref, vf_ref, gf_ref, qb_ref, ktb_ref, vb_ref, gb_ref, fb_ref, c0_ref, m0_ref,
                  hf_ref, hb_ref, c1_ref, m1_ref, c_s, m_s, *, nc):
    L, d = ML_CHUNK, ML_DH
    step = pl.program_id(1)

    @pl.when(step == 0)
    def _load_state():
        c_s[...] = c0_ref[0]
        m_s[...] = m0_ref[0]

    lane = lax.broadcasted_iota(jnp.int32, (L, LANES), 1)
    is_forget = (lane & ML_HEADS) != 0
    ti = lax.broadcasted_iota(jnp.int32, (L, L), 0)
    si = lax.broadcasted_iota(jnp.int32, (L, L), 1)
    ones = jnp.ones((L, d), BF16)
    dirs = ((False, qf_ref, ktf_ref, vf_ref, gf_ref, hf_ref), (True, qb_ref, ktb_ref, vb_ref, gb_ref, hb_ref))
    for di, (reverse, q_ref, kt_ref, v_ref, g_ref, h_ref) in enumerate(dirs):
        gates = g_ref[0]
        g2 = jnp.where(is_forget, _log_sigmoid(gates + fb_ref[...]), gates)
        a_cum = _scan_rows(jnp.where(is_forget, g2, 0.0), reverse, jnp.add, 0.0)
        b_t = pltpu.roll(g2, ML_HEADS, 1) - a_cum
        last = 0 if reverse else L - 1
        a_tot = a_cum[last:last + 1, :]
        m_old = m_s[di]
        mm = jnp.maximum(m_old, jnp.max(b_t, axis=0, keepdims=True))
        w_s_t = jnp.exp(b_t - mm).T
        decay = jnp.exp(m_old - mm)
        m_s[di] = a_tot + mm
        big_m = jnp.maximum(m_old, _scan_rows(b_t, reverse, jnp.maximum, MASK_VALUE))
        m_t = a_cum + big_m
        b_rows = b_t.T
        causal = (ti <= si) if reverse else (ti >= si)
        base = 2 * ML_HEADS * di + ML_HEADS
        for hd in range(ML_HEADS):
            f = base + hd
            ch = ML_HEADS * di + hd
            q = q_ref[0, :, d * hd:d * (hd + 1)]
            kt = kt_ref[0, d * hd:d * (hd + 1), :]
            vext = jnp.concatenate([v_ref[0, :, d * hd:d * (hd + 1)], ones], axis=-1)
            c_old = c_s[ch]
            big_m_b = jnp.broadcast_to(big_m[:, f:f + 1], (L, L))
            m_t_b = jnp.broadcast_to(m_t[:, f:f + 1], (L, d))
            m_old_b = jnp.broadcast_to(m_old[:, f:f + 1], (L, d))
            w_ts = jnp.exp(jnp.where(causal, b_rows[f:f + 1, :] - big_m_b, MASK_VALUE)) * _dot(q, kt)
            w_c = jnp.exp(m_old_b - big_m_b)
            lhs = jnp.concatenate([w_ts.astype(BF16), (q.astype(F32) * w_c).astype(BF16)], axis=-1)
            rhs = jnp.concatenate([vext, c_old.astype(BF16)], axis=0)
            out = _dot(lhs, rhs)
            h_ref[0, :, d * hd:d * (hd + 1)] = out[:, 0:d] / jnp.maximum(jnp.abs(out[:, d:]), jnp.exp(-m_t_b))
            kw_t = (kt.astype(F32) * w_s_t[f:f + 1, :]).astype(BF16)
            c_s[ch] = decay[:, f:f + 1] * c_old + _dot(kw_t, vext)

    @pl.when(step == nc - 1)
    def _store_state():
        c1_ref[0] = c_s[...]
        m1_ref[0] = m_s[...]


def _mlstm(q, kt, v, gates, fb_lane, state):
    bsz, n, _ = q.shape
    nc = n // ML_CHUNK
    L, w = ML_CHUNK, ML_WIDTH
    fwd = lambda b_, s: (b_, s, 0)
    bwd = lambda b_, s: (b_, nc - 1 - s, 0)
    fwd_t = lambda b_, s: (b_, 0, s)
    bwd_t = lambda b_, s: (b_, 0, nc - 1 - s)
    st = lambda b_, s: (b_, 0, 0, 0)
    c0, m0 = state
    c_shape, m_shape = (2 * ML_HEADS, ML_DH, 2 * ML_DH), (2, 1, LANES)
    tok = lambda width, im: pl.BlockSpec((1, L, width), im)
    out = pl.pallas_call(
        functools.partial(_mlstm_kernel, nc=nc),
        grid=(bsz, nc),
        in_specs=[tok(w, fwd), pl.BlockSpec((1, w, L), fwd_t), tok(w, fwd), tok(LANES, fwd),
                  tok(w, bwd), pl.BlockSpec((1, w, L), bwd_t), tok(w, bwd), tok(LANES, bwd),
                  pl.BlockSpec((1, LANES), lambda b_, s: (0, 0)),
                  pl.BlockSpec((1,) + c_shape, st), pl.BlockSpec((1,) + m_shape, st)],
        out_specs=[tok(w, fwd), tok(w, bwd), pl.BlockSpec((1,) + c_shape, st), pl.BlockSpec((1,) + m_shape, st)],
        out_shape=[jax.ShapeDtypeStruct((bsz, n, w), F32), jax.ShapeDtypeStruct((bsz, n, w), F32),
                   jax.ShapeDtypeStruct((bsz,) + c_shape, F32), jax.ShapeDtypeStruct((bsz,) + m_shape, F32)],
        scratch_shapes=[pltpu.VMEM(c_shape, F32), pltpu.VMEM(m_shape, F32)],
        compiler_params=_cparams(("parallel", "arbitrary")),
        name="mlstm",
    )(q, kt, v, gates, q, kt, v, gates, fb_lane, c0, m0)
    return out[0], out[1], (out[2], out[3])


def _na_kernel(*refs, n_blocks):
    if n_blocks:
        q_ref, kc_ref, vc_ref, kp_ref, k0_ref, k1_ref, vp_ref, v0_ref, v1_ref, bias_ref, o_ref = refs
        k_lat, v_lat = (kp_ref, k0_ref, k1_ref), (vp_ref, v0_ref, v1_ref)
    else:
        q_ref, kc_ref, vc_ref, o_ref = refs
        k_lat, v_lat = (), ()
    qb = q_ref.shape[2]
    row = lax.broadcasted_iota(jnp.int32, (LANES, qb), 0)

    def scores(hd):
        g, half = hd // 2, hd % 2
        q2 = q_ref[0, LANES * g:LANES * (g + 1), :]
        q = jnp.where((row >> 6) == half, q2, jnp.zeros_like(q2))
        sl = slice(LANES * g, LANES * (g + 1))
        parts = [_dot(kr[0, :, sl], q) for kr in k_lat]
        if n_blocks:
            bias = bias_ref[0, hd]
            parts = [p + bias[qb * t:qb * (t + 1), :] for t, p in enumerate(parts)]
        parts.append(_dot(kc_ref[0, :, sl], q))
        return parts

    def finish(hd, parts):
        g, half = hd // 2, hd % 2
        mx = functools.reduce(jnp.maximum, [jnp.max(p, axis=0, keepdims=True) for p in parts])
        rows = slice(VN_ROWS * g, VN_ROWS * (g + 1))
        acc = None
        for p, vr in zip(parts, v_lat + (vc_ref,)):
            t = _dot(vr[0, rows, :], jnp.exp2(p - mx).astype(BF16))
            acc = t if acc is None else acc + t
        return acc[NA_DH * half:NA_DH * (half + 1), :] / acc[LANES:LANES + 1, :]

    nxt = scores(0)
    halves = []
    for hd in range(NA_HEADS):
        cur = nxt
        if hd + 1 < NA_HEADS:
            nxt = scores(hd + 1)
        halves.append(finish(hd, cur))
        if hd % 2 == 1:
            g = hd // 2
            o_ref[0, :, LANES * g:LANES * (g + 1)] = jnp.concatenate(halves, axis=0).T.astype(BF16)
            halves = []


def _na_attn(qnt, kn, vnt, kc, vct, bias, qb):
    bsz, w, n = qnt.shape
    t = kc.shape[1]
    n_blocks = 0 if bias is None else 3
    nb = n // qb
    vr = vct.shape[1]
    tok = lambda off: pl.BlockSpec((1, qb, w), lambda b_, i: (b_, jnp.clip(i + off, 0, nb - 1), 0))
    feat = lambda r, off: pl.BlockSpec((1, r, qb), lambda b_, i: (b_, 0, jnp.clip(i + off, 0, nb - 1)))
    ins = [qnt, kc, vct]
    specs = [feat(w, 0), pl.BlockSpec((1, t, w), lambda b_, i: (b_, 0, 0)), pl.BlockSpec((1, vr, t), lambda b_, i: (b_, 0, 0))]
    if n_blocks:
        ins += [kn, kn, kn, vnt, vnt, vnt, bias]
        block_class = lambda b_, i: (jnp.where(i == 0, 0, jnp.where(i == nb - 1, 2, 1)), 0, 0, 0)
        specs += [tok(-1), tok(0), tok(1), feat(vr, -1), feat(vr, 0), feat(vr, 1),
                  pl.BlockSpec((1,) + bias.shape[1:], block_class)]
    return pl.pallas_call(
        functools.partial(_na_kernel, n_blocks=n_blocks),
        grid=(bsz, nb),
        in_specs=specs,
        out_specs=pl.BlockSpec((1, qb, w), lambda b_, i: (b_, i, 0)),
        out_shape=jax.ShapeDtypeStruct((bsz, n, w), BF16),
        compiler_params=_cparams(("parallel", "parallel")),
        name="na_attn" if n_blocks else "ctx_attn",
    )(*ins)


def _take_cols(w, idx):
    idx = np.asarray(idx)
    cols = jnp.take(w, jnp.asarray(np.maximum(idx, 0)), axis=-1)
    return jnp.where(jnp.asarray(idx >= 0), cols, jnp.zeros_like(cols))


def _even_col_index():
    idx = []
    half = DA_QK // 2
    for o in (0, 256, 512, 768):
        for part in (0, half):
            idx += [o + DA_QK * hd + part + i for hd in range(DA_HEADS) for i in range(half)]
    idx += list(range(1024, 1536)) + list(range(1536, 1792)) + list(range(1792, 1920))
    kr, hr = 1920, MLA_ROPE // 2
    idx += [kr + i for i in range(MLA_ROPE)] + [-1] * (LANES - MLA_ROPE)
    idx += [kr + hr + i for i in range(hr)] + [kr + i for i in range(hr)] + [-1] * (LANES - MLA_ROPE)
    idx += list(range(1952, 2976))
    assert len(idx) == EV_COLS
    return idx


def _uq_col_index():
    per = MLA_NOPE + MLA_ROPE
    hr = MLA_ROPE // 2
    idx = [per * hd + i for hd in range(MLA_HEADS) for i in range(MLA_NOPE)]
    idx += [per * hd + MLA_NOPE + i for hd in range(MLA_HEADS) for i in range(hr)]
    idx += [per * hd + MLA_NOPE + hr + i for hd in range(MLA_HEADS) for i in range(hr)]
    return idx


def _ukv_col_index():
    per = MLA_NOPE + MLA_V
    idx = [per * hd + i for hd in range(MLA_HEADS) for i in range(MLA_NOPE)]
    idx += [per * hd + MLA_NOPE + i for hd in range(MLA_HEADS) for i in range(MLA_V)]
    return idx


def _odd_col_index():
    idx = list(range(0, 2048)) + list(range(2048, 2064)) + [-1] * (LANES - 16) + list(range(2064, 4624))
    assert len(idx) == OD_COLS
    return idx


def _rope_tables(n_lat, n_ctx):
    t = jnp.arange(n_lat)
    row = (t // GRID_W).astype(F32)
    col = (t % GRID_W).astype(F32)

    def cs(dim):
        nf = dim // 4
        freqs = ROPE_BASE ** (-jnp.arange(nf, dtype=F32) / nf)
        ang = jnp.concatenate([row[:, None] * freqs, col[:, None] * freqs], axis=-1)
        return jnp.cos(ang), jnp.sin(ang)

    ca, sa = cs(DA_QK)
    cm, sm = cs(MLA_ROPE)
    lat = jnp.concatenate([jnp.tile(ca, (1, 4)), jnp.tile(sa, (1, 4)), jnp.tile(cm, (1, 8)), jnp.tile(sm, (1, 8)),
                           jnp.tile(jnp.concatenate([-sm, sm], axis=-1), (1, 4))], axis=-1)
    one, zero = jnp.ones((n_ctx, LANES), F32), jnp.zeros((n_ctx, LANES), F32)
    return lat, jnp.concatenate([one, zero, one, zero, zero], axis=-1)


def _na_bias_tiles(rpb, qb, rows):
    heads, n_dr, n_dc = rpb.shape
    rows_per = qb // GRID_W
    assert rows >= 3 * rows_per and NA_ROWS == 2 * rows_per
    kc, qc = np.arange(GRID_W)[:, None], np.arange(GRID_W)[None, :]
    c0 = np.clip(qc - NA_COLS // 2, 0, GRID_W - NA_COLS)
    col_ok = (kc >= c0) & (kc < c0 + NA_COLS)
    onehot = ((kc - qc + NA_COLS - 1)[None] == np.arange(n_dc)[:, None, None]) & col_ok[None]
    tiles = jnp.einsum('hrj,jp->hrp', rpb * LOG2E, jnp.asarray(onehot.reshape(n_dc, -1), F32),
                       precision=lax.Precision.HIGHEST).reshape(heads, n_dr, GRID_W, GRID_W)
    tiles = jnp.where(jnp.asarray(col_ok)[None, None], tiles, MASK_VALUE)
    kr = (np.arange(3 * rows_per) - rows_per)[:, None]
    qr = np.arange(rows_per)[None, :]
    dr = kr - qr
    idx = np.clip(dr + NA_ROWS - 1, 0, n_dr - 1).reshape(-1)
    blocks = jnp.take(tiles, jnp.asarray(idx), axis=1).reshape(heads, 3 * rows_per, rows_per, GRID_W, GRID_W)
    window = [(kr >= 0) & (kr < NA_ROWS) & (qr >= 0),
              (dr >= -(NA_ROWS // 2)) & (dr < NA_ROWS // 2),
              (kr >= rows_per - NA_ROWS) & (kr < rows_per) & (qr >= 0)]
    out = [jnp.where(jnp.asarray(ok)[None, :, :, None, None], blocks, MASK_VALUE) for ok in window]
    return jnp.stack(out).transpose(0, 1, 2, 4, 3, 5).reshape(3, heads, 3 * qb, qb)


def _tile_plan(n_lat, n_ctx):
    return {
        "tok_lat": 512,
        "tok_odd": 256,
        "tok_ctx": min(256, n_ctx),
        "attn_q": 1024 if n_lat % 1024 == 0 else 512,
        "attn_k": 512,
        "na_q": 4 * GRID_W,
    }


def kernel(x, c, ctx, c_ctx, ada_w, ada_b, ln_g, ln_b, ev_w_in, ev_b_in, da_lambda, da_subln_g, mla_q_norm_g, mla_kv_norm_g, mla_w_uq, mla_w_ukv, ev_w_out, od_w_in, od_b_in, ml_conv_w, ml_conv_b, ml_f_bias, ml_norm_g, na_rpb, od_w_out):
    bsz, n_lat, d = x.shape
    n_ctx = ctx.shape[1]
    rows = n_lat // GRID_W
    assert d == D_MODEL and bsz + 1 <= 8 and n_lat % 512 == 0 and n_ctx % 256 == 0 and rows >= NA_ROWS

    cond8 = jnp.zeros((8, d), F32).at[:bsz].set(c).at[bsz].set(c_ctx)
    mod_all = _ada_mod(cond8, ada_w, ada_b)
    rope_lat, rope_ctx = _rope_tables(n_lat, n_ctx)

    tp = _tile_plan(n_lat, n_ctx)
    tm_lat, tm_ctx, tq, tk, na_qb = tp["tok_lat"], tp["tok_ctx"], tp["attn_q"], tp["attn_k"], tp["na_q"]
    xl, xc = x, ctx
    for l in range(DEPTH):
        update_ctx = l < DEPTH - 1
        i = l // 2
        mod_l = mod_all[l, :bsz].reshape(bsz, 1, 3 * d)
        mod_c = mod_all[l, bsz].reshape(1, 1, 3 * d)
        lng, lnb = ln_g[l].reshape(1, d), ln_b[l].reshape(1, d)
        if l % 2 == 0:
            lam_init = 0.8 - 0.6 * math.exp(-0.3 * l)
            w = _take_cols(ev_w_in[i], _even_col_index()).astype(BF16)
            b = _take_cols(ev_b_in[i], _even_col_index()).reshape(1, EV_COLS)
            wuq = _take_cols(mla_w_uq[i], _uq_col_index()).astype(BF16)
            wukv = _take_cols(mla_w_ukv[i], _ukv_col_index()).astype(BF16)
            qg = mla_q_norm_g[i].reshape(1, MLA_Q_RANK)
            kvg = mla_kv_norm_g[i].reshape(1, MLA_KV_RANK)
            subg = da_subln_g[i].reshape(DA_V, 1)
            wout = ev_w_out[i].astype(BF16)
            qt_l, ka_l, km_l, vt_l, g_l = _proj_even(xl, mod_l, rope_lat, w, b, qg, kvg, wuq, wukv, tm_lat)
            qt_c, ka_c, km_c, vt_c, g_c = _proj_even(xc, mod_c, rope_ctx, w, b, qg, kvg, wuq, wukv, tm_ctx)
            y_l = _attn_even(da_lambda[i], subg, qt_l, ka_l, km_l, vt_l, (ka_c, km_c, vt_c), lam_init, tq, tk)
            xl_new = _out_proj(_out_even_kernel, "out_even", (y_l, g_l), xl, mod_l, wout, (), lng, lnb, tm_lat)
            if update_ctx:
                y_c = _attn_even(da_lambda[i], subg, qt_c, ka_c, km_c, vt_c, None, lam_init, n_ctx, n_ctx)
                xc = _out_proj(_out_even_kernel, "out_even", (y_c, g_c), xc, mod_c, wout, (), lng, lnb, tm_ctx)
            xl = xl_new
        else:
            w = _take_cols(od_w_in[i], _odd_col_index()).astype(BF16)
            b = _take_cols(od_b_in[i], _odd_col_index()).reshape(1, OD_COLS)
            wout = od_w_out[i].astype(BF16)
            cw, cb = ml_conv_w[i], ml_conv_b[i].reshape(1, 2 * ML_WIDTH)
            fb = ml_f_bias[i].astype(F32)
            fb_lane = jnp.zeros((1, LANES), F32).at[0, 4:8].set(fb[0]).at[0, 12:16].set(fb[1])
            ng = ml_norm_g[i].reshape(1, ML_WIDTH)
            qk_l, v_l, so_l, gt_l, qn_l, kn_l, vn_l, g_l = _proj_odd(xl, mod_l, w, b, tp["tok_odd"])
            qk_c, v_c, so_c, gt_c, qn_c, kn_c, vn_c, g_c = _proj_odd(xc, mod_c, w, b, tm_ctx)
            q_l, kt_l = _conv_silu(qk_l, cw, cb, tp["tok_lat"])
            q_c, kt_c = _conv_silu(qk_c, cw, cb, n_ctx)
            zero = (jnp.zeros((bsz, 2 * ML_HEADS, ML_DH, 2 * ML_DH), F32), jnp.zeros((bsz, 2, 1, LANES), F32))
            hf_c, hb_c, st_c = _mlstm(q_c, kt_c, v_c, gt_c, fb_lane, zero)
            hf_l, hb_l, _ = _mlstm(q_l, kt_l, v_l, gt_l, fb_lane, st_c)
            bias = _na_bias_tiles(na_rpb[i].astype(F32), na_qb, rows)
            na_l = _na_attn(qn_l, kn_l, vn_l, kn_c, vn_c, bias, na_qb)
            xl_new = _out_proj(_out_odd_kernel, "out_odd", (hf_l, hb_l, so_l, na_l, g_l), xl, mod_l, wout, (ng,), lng, lnb, tm_lat)
            if update_ctx:
                na_c = _na_attn(qn_c, None, None, kn_c, vn_c, None, n_ctx)
                xc = _out_proj(_out_odd_kernel, "out_odd", (hf_c, hb_c, so_c, na_c, g_c), xc, mod_c, wout, (ng,), lng, lnb, tm_ctx)
            xl = xl_new
    return xl
```

```python
import functools
import math

import numpy as np
import jax
import jax.numpy as jnp
from jax import lax
from jax.experimental import pallas as pl
from jax.experimental.pallas import tpu as pltpu

F32 = jnp.float32
BF16 = jnp.bfloat16

D_MODEL = 1024
DEPTH = 4
GRID_W = 64
ROPE_BASE = 10000.0
LN_EPS = 1e-5
RMS_EPS = 1e-6
DEEPNORM_ALPHA = (2.0 * DEPTH) ** 0.25

DA_HEADS, DA_QK, DA_V = 4, 64, 128
LOG2E = math.log2(math.e)
DA_SCALE = DA_QK ** -0.5 * LOG2E
MLA_HEADS, MLA_Q_RANK, MLA_KV_RANK, MLA_NOPE, MLA_ROPE, MLA_V = 8, 256, 128, 64, 32, 64
MLA_SCALE = (MLA_NOPE + MLA_ROPE) ** -0.5 * LOG2E
ML_HEADS, ML_DH, ML_CONV, ML_CHUNK = 4, 128, 5, 128
ML_WIDTH = ML_HEADS * ML_DH
NA_HEADS, NA_DH, NA_ROWS, NA_COLS = 8, 64, 8, 16
NA_WIDTH = NA_HEADS * NA_DH
NA_SCALE = NA_DH ** -0.5 * LOG2E

LANES = 128
MASK_VALUE = -1e30
VMEM_LIMIT = 48 * 1024 * 1024

EV_COLS = 3200
EV_V, EV_CQ, EV_CKV, EV_KRA, EV_KRB, EV_GATE = 1024, 1536, 1792, 1920, 2048, 2176
OD_COLS = 4736
OD_V, OD_O, OD_GATES, OD_QN, OD_KN, OD_VN, OD_GATE = 1024, 1536, 2048, 2176, 2688, 3200, 3712
ONES_ROWS = 16
VA_ROWS = DA_V + ONES_ROWS
VM_ROWS = MLA_V + ONES_ROWS
VT_ROWS = DA_HEADS * VA_ROWS + MLA_HEADS * VM_ROWS
QT_ROWS = 1280
N_UNITS = 16
VN_ROWS = LANES + ONES_ROWS


def _cparams(sem):
    return pltpu.CompilerParams(dimension_semantics=sem, vmem_limit_bytes=VMEM_LIMIT)


def _silu(x):
    return x * jax.nn.sigmoid(x)


def _dot(a, b):
    return jnp.dot(a, b, preferred_element_type=F32)


def _log_sigmoid(x):
    return jnp.minimum(x, 0.0) - jnp.log(1.0 + jnp.exp(-jnp.abs(x)))


def _ada_kernel(c_ref, w_ref, b_ref, o_ref):
    c = c_ref[...]
    o_ref[0] = _dot(_silu(c).astype(BF16), w_ref[0].astype(BF16)) + b_ref[0]


def _ada_mod(cond8, ada_w, ada_b):
    depth, d, e = ada_w.shape
    tn = 1024
    return pl.pallas_call(
        _ada_kernel,
        grid=(depth, e // tn),
        in_specs=[pl.BlockSpec((8, d), lambda l, n: (0, 0)),
                  pl.BlockSpec((1, d, tn), lambda l, n: (l, 0, n)),
                  pl.BlockSpec((1, 1, tn), lambda l, n: (l, 0, n))],
        out_specs=pl.BlockSpec((1, 8, tn), lambda l, n: (l, 0, n)),
        out_shape=jax.ShapeDtypeStruct((depth, 8, e), F32),
        compiler_params=_cparams(("parallel", "parallel")),
        name="ada_mod",
    )(cond8, ada_w, ada_b.reshape(depth, 1, e))


def _mod_spec(mod):
    per_batch = mod.shape[0] > 1
    return pl.BlockSpec((1, 1, mod.shape[2]), (lambda b, i: (b, 0, 0)) if per_batch else (lambda b, i: (0, 0, 0)))


def _proj_even_kernel(x_ref, mod_ref, rope_ref, w_ref, b_ref, qg_ref, kvg_ref, wuq_ref, wukv_ref,
                      qt_ref, ka_ref, km_ref, vt_ref, g_ref):
    d = D_MODEL
    x = x_ref[0]
    mod = mod_ref[0]
    h = (x * (1.0 + mod[:, d:2 * d]) + mod[:, 0:d]).astype(BF16)
    z = _dot(h, w_ref[...]) + b_ref[...]
    rope = rope_ref[...]
    cos_a, sin_a, cos_m, sin_m, sin_ms = (rope[:, LANES * i:LANES * (i + 1)] for i in range(5))

    def blk(o):
        return z[:, o:o + LANES]

    def rot(a, b, c, s):
        return a * c - b * s, b * c + a * s

    q1a, q1b = rot(blk(0), blk(128), cos_a, sin_a)
    q2a, q2b = rot(blk(256), blk(384), cos_a, sin_a)
    k1a, k1b = rot(blk(512), blk(640), cos_a, sin_a)
    k2a, k2b = rot(blk(768), blk(896), cos_a, sin_a)
    for i, t in enumerate((q1a, q1b, q2a, q2b)):
        qt_ref[0, LANES * i:LANES * (i + 1), :] = (t * DA_SCALE).T.astype(BF16)
    for i, t in enumerate((k1a, k1b, k2a, k2b)):
        ka_ref[0, i // 2, :, LANES * (i % 2):LANES * (i % 2 + 1)] = t.astype(BF16)
    for i in range(4):
        vt_ref[0, VA_ROWS * i:VA_ROWS * i + DA_V, :] = blk(EV_V + LANES * i).T.astype(BF16)
        vt_ref[0, VA_ROWS * i + DA_V:VA_ROWS * (i + 1), :] = jnp.ones((ONES_ROWS, x.shape[0]), BF16)

    def rms(t, g):
        return (t * lax.rsqrt(jnp.mean(t * t, axis=-1, keepdims=True) + RMS_EPS) * g).astype(BF16)

    qm = _dot(rms(z[:, EV_CQ:EV_CQ + MLA_Q_RANK], qg_ref[...]), wuq_ref[...])
    qra, qrb = rot(qm[:, 512:640], qm[:, 640:768], cos_m, sin_m)
    for i in range(4):
        qt_ref[0, 512 + LANES * i:512 + LANES * (i + 1), :] = (qm[:, LANES * i:LANES * (i + 1)] * MLA_SCALE).T.astype(BF16)
    qt_ref[0, 1024:1152, :] = (qra * MLA_SCALE).T.astype(BF16)
    qt_ref[0, 1152:1280, :] = (qrb * MLA_SCALE).T.astype(BF16)

    kvm = _dot(rms(z[:, EV_CKV:EV_CKV + MLA_KV_RANK], kvg_ref[...]), wukv_ref[...])
    k_rope = (blk(EV_KRA) * cos_m + blk(EV_KRB) * sin_ms).astype(BF16)
    for g in range(4):
        km_ref[0, g, :, 0:LANES] = kvm[:, LANES * g:LANES * (g + 1)].astype(BF16)
        km_ref[0, g, :, LANES:2 * LANES] = k_rope
        vm_t = kvm[:, 512 + LANES * g:512 + LANES * (g + 1)].T.astype(BF16)
        for half in range(2):
            r0 = DA_HEADS * VA_ROWS + VM_ROWS * (2 * g + half)
            vt_ref[0, r0:r0 + MLA_V, :] = vm_t[MLA_V * half:MLA_V * (half + 1), :]
            vt_ref[0, r0 + MLA_V:r0 + VM_ROWS, :] = jnp.ones((ONES_ROWS, x.shape[0]), BF16)

    g_ref[0] = _silu(z[:, EV_GATE:EV_GATE + d]).astype(BF16)


def _proj_even(x, mod, rope, w, b, qg, kvg, wuq, wukv, tm):
    bsz, n, d = x.shape
    const = lambda shape: pl.BlockSpec(shape, lambda b_, i: (0,) * len(shape))
    return pl.pallas_call(
        _proj_even_kernel,
        grid=(bsz, n // tm),
        in_specs=[pl.BlockSpec((1, tm, d), lambda b_, i: (b_, i, 0)),
                  _mod_spec(mod),
                  pl.BlockSpec((tm, 5 * LANES), lambda b_, i: (i, 0)),
                  const((d, EV_COLS)), const((1, EV_COLS)), const((1, MLA_Q_RANK)), const((1, MLA_KV_RANK)),
                  const((MLA_Q_RANK, 768)), const((MLA_KV_RANK, 1024))],
        out_specs=[pl.BlockSpec((1, QT_ROWS, tm), lambda b_, i: (b_, 0, i)),
                   pl.BlockSpec((1, 2, tm, 256), lambda b_, i: (b_, 0, i, 0)),
                   pl.BlockSpec((1, 4, tm, 256), lambda b_, i: (b_, 0, i, 0)),
                   pl.BlockSpec((1, VT_ROWS, tm), lambda b_, i: (b_, 0, i)),
                   pl.BlockSpec((1, tm, d), lambda b_, i: (b_, i, 0))],
        out_shape=[jax.ShapeDtypeStruct((bsz, QT_ROWS, n), BF16),
                   jax.ShapeDtypeStruct((bsz, 2, n, 256), BF16),
                   jax.ShapeDtypeStruct((bsz, 4, n, 256), BF16),
                   jax.ShapeDtypeStruct((bsz, VT_ROWS, n), BF16),
                   jax.ShapeDtypeStruct((bsz, n, d), BF16)],
        compiler_params=_cparams(("parallel", "parallel")),
        name="proj_even",
    )(x, mod, rope, w, b, qg, kvg, wuq, wukv)


def _attn_even_kernel(*refs, tq, nkv, sub_tk, has_ctx, lam_init):
    if has_ctx:
        (lam_ref, subg_ref, qt_ref, ka_ref, km_ref, vt_ref, kac_ref, kmc_ref, vtc_ref,
         y_ref, qw, acc_a, acc_m, m_s) = refs
    else:
        (lam_ref, subg_ref, qt_ref, ka_ref, km_ref, vt_ref,
         y_ref, qw, acc_a, acc_m, m_s) = refs
    j = pl.program_id(2)

    def process(ka, km, vt, n_sub, sub):
        def scores(t, u):
            rows = slice(sub * t, sub * (t + 1))
            k = ka[0, u // DA_HEADS, rows, :] if u < 2 * DA_HEADS else km[0, (u - 2 * DA_HEADS) // 2, rows, :]
            return _dot(k, qw[u])

        def softmax_step(u, s):
            m_old = m_s[u]
            m_new = jnp.maximum(m_old, jnp.max(s, axis=0, keepdims=True))
            alpha = jnp.exp2(m_old - m_new)
            p = jnp.exp2(s - m_new)
            m_s[u] = m_new
            return alpha, p.astype(BF16)

        def pv(t, u, alpha, p):
            cols = slice(sub * t, sub * (t + 1))
            if u < 2 * DA_HEADS:
                hd = u % DA_HEADS
                acc_a[u] = alpha * acc_a[u] + _dot(vt[0, VA_ROWS * hd:VA_ROWS * (hd + 1), cols], p)
            else:
                hd = u - 2 * DA_HEADS
                v = vt[0, DA_HEADS * VA_ROWS + VM_ROWS * hd:DA_HEADS * VA_ROWS + VM_ROWS * (hd + 1), cols]
                acc_m[hd] = alpha * acc_m[hd] + _dot(v, p)

        pairs = [(t, u) for t in range(n_sub) for u in range(N_UNITS)]
        s_next = scores(*pairs[0])
        pending = None
        for i, (t, u) in enumerate(pairs):
            s_cur = s_next
            if i + 1 < len(pairs):
                s_next = scores(*pairs[i + 1])
            if pending is not None:
                pv(*pending)
            alpha, p = softmax_step(u, s_cur)
            pending = (t, u, alpha, p)
        pv(*pending)

    @pl.when(j == 0)
    def _init():
        row = lax.broadcasted_iota(jnp.int32, (256, tq), 0)
        head_of_row = (row >> 5) & (DA_HEADS - 1)
        for mp in range(2):
            q = qt_ref[0, 256 * mp:256 * (mp + 1), :]
            for hd in range(DA_HEADS):
                qw[mp * DA_HEADS + hd] = jnp.where(head_of_row == hd, q, jnp.zeros_like(q))
        for hd in range(MLA_HEADS):
            u = 2 * DA_HEADS + hd
            qw[u] = jnp.zeros((256, tq), BF16)
            qw[u, 64 * (hd % 2):64 * (hd % 2) + 64, :] = qt_ref[0, 512 + 64 * hd:512 + 64 * hd + 64, :]
            qw[u, 128:144, :] = qt_ref[0, 1024 + 16 * hd:1024 + 16 * hd + 16, :]
            qw[u, 144:160, :] = qt_ref[0, 1152 + 16 * hd:1152 + 16 * hd + 16, :]
        acc_a[...] = jnp.zeros(acc_a.shape, F32)
        acc_m[...] = jnp.zeros(acc_m.shape, F32)
        m_s[...] = jnp.full(m_s.shape, MASK_VALUE, F32)
        if has_ctx:
            process(kac_ref, kmc_ref, vtc_ref, 1, kac_ref.shape[2])

    process(ka_ref, km_ref, vt_ref, ka_ref.shape[2] // sub_tk, sub_tk)

    @pl.when(j == nkv - 1)
    def _finish():
        lm = lam_ref[...]
        lam = (jnp.exp(jnp.sum(lm[0:1] * lm[1:2], axis=-1, keepdims=True))
               - jnp.exp(jnp.sum(lm[2:3] * lm[3:4], axis=-1, keepdims=True)) + lam_init)
        for hd in range(DA_HEADS):
            a1, a2 = acc_a[hd], acc_a[DA_HEADS + hd]
            o = a1[0:DA_V] / a1[DA_V:DA_V + 1] - lam * (a2[0:DA_V] / a2[DA_V:DA_V + 1])
            o = o * lax.rsqrt(jnp.mean(o * o, axis=0, keepdims=True) + RMS_EPS) * subg_ref[...] * (1.0 - lam_init)
            y_ref[0, :, DA_V * hd:DA_V * (hd + 1)] = o.T.astype(BF16)
        for g in range(MLA_HEADS // 2):
            b1, b2 = acc_m[2 * g], acc_m[2 * g + 1]
            o = jnp.concatenate([b1[0:MLA_V] / b1[MLA_V:MLA_V + 1], b2[0:MLA_V] / b2[MLA_V:MLA_V + 1]], axis=0)
            y_ref[0, :, 512 + LANES * g:512 + LANES * (g + 1)] = o.T.astype(BF16)


def _attn_even(lam_p, subg, qt, ka, km, vt, ctx_kv, lam_init, tq, tk):
    bsz, _, n = qt.shape
    nk = ka.shape[2]
    nq, nkv = n // tq, nk // tk
    has_ctx = ctx_kv is not None
    in_specs = [pl.BlockSpec((4, DA_QK), lambda b, i, j: (0, 0)),
                pl.BlockSpec((DA_V, 1), lambda b, i, j: (0, 0)),
                pl.BlockSpec((1, QT_ROWS, tq), lambda b, i, j: (b, 0, i)),
                pl.BlockSpec((1, 2, tk, 256), lambda b, i, j: (b, 0, j, 0)),
                pl.BlockSpec((1, 4, tk, 256), lambda b, i, j: (b, 0, j, 0)),
                pl.BlockSpec((1, VT_ROWS, tk), lambda b, i, j: (b, 0, j))]
    args = [lam_p, subg, qt, ka, km, vt]
    if has_ctx:
        tc = ctx_kv[0].shape[2]
        in_specs += [pl.BlockSpec((1, 2, tc, 256), lambda b, i, j: (b, 0, 0, 0)),
                     pl.BlockSpec((1, 4, tc, 256), lambda b, i, j: (b, 0, 0, 0)),
                     pl.BlockSpec((1, VT_ROWS, tc), lambda b, i, j: (b, 0, 0))]
        args += list(ctx_kv)
    return pl.pallas_call(
        functools.partial(_attn_even_kernel, tq=tq, nkv=nkv, sub_tk=min(tk, 512), has_ctx=has_ctx, lam_init=lam_init),
        grid=(bsz, nq, nkv),
        in_specs=in_specs,
        out_specs=pl.BlockSpec((1, tq, D_MODEL), lambda b, i, j: (b, i, 0)),
        out_shape=jax.ShapeDtypeStruct((bsz, n, D_MODEL), BF16),
        scratch_shapes=[pltpu.VMEM((N_UNITS, 256, tq), BF16),
                        pltpu.VMEM((2 * DA_HEADS, VA_ROWS, tq), F32),
                        pltpu.VMEM((MLA_HEADS, VM_ROWS, tq), F32),
                        pltpu.VMEM((N_UNITS, 1, tq), F32)],
        compiler_params=_cparams(("parallel", "parallel", "arbitrary")),
        name="attn_even",
    )(*args)


def _layer_norm_rows(t, g, b):
    mu = jnp.mean(t, axis=-1, keepdims=True)
    tc = t - mu
    var = jnp.mean(tc * tc, axis=-1, keepdims=True)
    return tc * lax.rsqrt(var + LN_EPS) * g + b


def _out_even_kernel(y_ref, g_ref, x_ref, mod_ref, w_ref, lng_ref, lnb_ref, o_ref):
    d = D_MODEL
    y = (y_ref[0].astype(F32) * g_ref[0].astype(F32)).astype(BF16)
    out = _dot(y, w_ref[...])
    t = DEEPNORM_ALPHA * x_ref[0] + mod_ref[0][:, 2 * d:3 * d] * out
    o_ref[0] = _layer_norm_rows(t, lng_ref[...], lnb_ref[...])


def _out_odd_kernel(hf_ref, hb_ref, so_ref, na_ref, g_ref, x_ref, mod_ref, w_ref, ng_ref, lng_ref, lnb_ref, o_ref):
    d = D_MODEL
    hsum = hf_ref[0] + hb_ref[0]
    parts = []
    for hd in range(ML_HEADS):
        t = hsum[:, ML_DH * hd:ML_DH * (hd + 1)]
        mu = jnp.mean(t, axis=-1, keepdims=True)
        tc = t - mu
        var = jnp.mean(tc * tc, axis=-1, keepdims=True)
        parts.append(tc * lax.rsqrt(var + LN_EPS))
    y_ml = jnp.concatenate(parts, axis=-1) * ng_ref[...] * so_ref[0].astype(F32)
    g = g_ref[0].astype(F32)
    y = jnp.concatenate([y_ml * g[:, 0:ML_WIDTH], na_ref[0].astype(F32) * g[:, ML_WIDTH:]], axis=-1).astype(BF16)
    out = _dot(y, w_ref[...])
    t = DEEPNORM_ALPHA * x_ref[0] + mod_ref[0][:, 2 * d:3 * d] * out
    o_ref[0] = _layer_norm_rows(t, lng_ref[...], lnb_ref[...])


def _out_proj(kernel_fn, name, acts, x, mod, w, extra, ln_g, ln_b, tm):
    bsz, n, d = x.shape
    tok = lambda a: pl.BlockSpec((1, tm, a.shape[2]), lambda b_, i: (b_, i, 0))
    const = lambda a: pl.BlockSpec(a.shape, lambda b_, i: (0,) * a.ndim)
    ins = list(acts) + [x, mod, w] + list(extra) + [ln_g, ln_b]
    specs = [tok(a) for a in acts] + [tok(x), _mod_spec(mod), const(w)] + [const(e) for e in extra] + [const(ln_g), const(ln_b)]
    return pl.pallas_call(
        kernel_fn,
        grid=(bsz, n // tm),
        in_specs=specs,
        out_specs=pl.BlockSpec((1, tm, d), lambda b_, i: (b_, i, 0)),
        out_shape=jax.ShapeDtypeStruct((bsz, n, d), F32),
        compiler_params=_cparams(("parallel", "parallel")),
        name=name,
    )(*ins)


def _proj_odd_kernel(x_ref, mod_ref, w_ref, b_ref, qk_ref, v_ref, so_ref, gt_ref, qn_ref, kn_ref, vn_ref, g_ref):
    d = D_MODEL
    x = x_ref[0]
    mod = mod_ref[0]
    h = (x * (1.0 + mod[:, d:2 * d]) + mod[:, 0:d]).astype(BF16)
    z = _dot(h, w_ref[...]) + b_ref[...]
    qk_ref[0] = z[:, 0:2 * ML_WIDTH]
    v_ref[0] = z[:, OD_V:OD_V + ML_WIDTH].astype(BF16)
    so_ref[0] = jax.nn.sigmoid(z[:, OD_O:OD_O + ML_WIDTH]).astype(BF16)
    gt_ref[0] = z[:, OD_GATES:OD_GATES + LANES]
    kn_ref[0] = z[:, OD_KN:OD_KN + NA_WIDTH].astype(BF16)
    for g in range(NA_HEADS // 2):
        qn_ref[0, LANES * g:LANES * (g + 1), :] = (z[:, OD_QN + LANES * g:OD_QN + LANES * (g + 1)] * NA_SCALE).T.astype(BF16)
        vn_ref[0, VN_ROWS * g:VN_ROWS * g + LANES, :] = z[:, OD_VN + LANES * g:OD_VN + LANES * (g + 1)].T.astype(BF16)
        vn_ref[0, VN_ROWS * g + LANES:VN_ROWS * (g + 1), :] = jnp.ones((ONES_ROWS, x.shape[0]), BF16)
    g_ref[0] = _silu(z[:, OD_GATE:OD_GATE + d]).astype(BF16)


def _proj_odd(x, mod, w, b, tm):
    bsz, n, d = x.shape
    const = lambda shape: pl.BlockSpec(shape, lambda b_, i: (0,) * len(shape))
    tok = lambda wd, dt: (pl.BlockSpec((1, tm, wd), lambda b_, i: (b_, i, 0)), jax.ShapeDtypeStruct((bsz, n, wd), dt))
    feat = lambda r, dt: (pl.BlockSpec((1, r, tm), lambda b_, i: (b_, 0, i)), jax.ShapeDtypeStruct((bsz, r, n), dt))
    outs = [tok(2 * ML_WIDTH, F32), tok(ML_WIDTH, BF16), tok(ML_WIDTH, BF16), tok(LANES, F32),
            feat(NA_WIDTH, BF16), tok(NA_WIDTH, BF16), feat(VN_ROWS * NA_HEADS // 2, BF16), tok(d, BF16)]
    return pl.pallas_call(
        _proj_odd_kernel,
        grid=(bsz, n // tm),
        in_specs=[pl.BlockSpec((1, tm, d), lambda b_, i: (b_, i, 0)), _mod_spec(mod),
                  const((d, OD_COLS)), const((1, OD_COLS))],
        out_specs=[o[0] for o in outs],
        out_shape=[o[1] for o in outs],
        compiler_params=_cparams(("parallel", "parallel")),
        name="proj_odd",
    )(x, mod, w, b)


def _conv_kernel(cur_ref, prev_ref, next_ref, w_ref, b_ref, q_ref, kt_ref, buf, *, tm, nt):
    i = pl.program_id(1)
    buf[0:8, :] = jnp.where(i > 0, prev_ref[0], 0.0)
    buf[8:8 + tm, :] = cur_ref[0]
    buf[8 + tm:16 + tm, :] = jnp.where(i < nt - 1, next_ref[0], 0.0)
    w = w_ref[...]
    acc = b_ref[...] + buf[6:6 + tm, :] * w[0:1]
    for tap in range(1, ML_CONV):
        acc = acc + buf[6 + tap:6 + tap + tm, :] * w[tap:tap + 1]
    y = _silu(acc)
    q_ref[0] = y[:, 0:ML_WIDTH].astype(BF16)
    for hd in range(ML_HEADS):
        k = y[:, ML_WIDTH + ML_DH * hd:ML_WIDTH + ML_DH * (hd + 1)] * (ML_DH ** -0.5)
        kt_ref[0, ML_DH * hd:ML_DH * (hd + 1), :] = k.T.astype(BF16)


def _conv_silu(qk, w, b, tm):
    bsz, n, c = qk.shape
    nt = n // tm
    r = tm // 8
    last8 = n // 8 - 1
    return pl.pallas_call(
        functools.partial(_conv_kernel, tm=tm, nt=nt),
        grid=(bsz, nt),
        in_specs=[pl.BlockSpec((1, tm, c), lambda b_, i: (b_, i, 0)),
                  pl.BlockSpec((1, 8, c), lambda b_, i: (b_, jnp.maximum(i * r - 1, 0), 0)),
                  pl.BlockSpec((1, 8, c), lambda b_, i: (b_, jnp.minimum((i + 1) * r, last8), 0)),
                  pl.BlockSpec((ML_CONV, c), lambda b_, i: (0, 0)),
                  pl.BlockSpec((1, c), lambda b_, i: (0, 0))],
        out_specs=[pl.BlockSpec((1, tm, ML_WIDTH), lambda b_, i: (b_, i, 0)),
                   pl.BlockSpec((1, ML_WIDTH, tm), lambda b_, i: (b_, 0, i))],
        out_shape=[jax.ShapeDtypeStruct((bsz, n, ML_WIDTH), BF16),
                   jax.ShapeDtypeStruct((bsz, ML_WIDTH, n), BF16)],
        scratch_shapes=[pltpu.VMEM((tm + 16, c), F32)],
        compiler_params=_cparams(("parallel", "parallel")),
        name="conv_silu",
    )(qk, qk, qk, w, b)


def _scan_rows(x, reverse, op, fill):
    n = x.shape[0]
    row = lax.broadcasted_iota(jnp.int32, x.shape, 0)
    sh = 1
    while sh < n:
        if reverse:
            x = op(x, jnp.where(row + sh < n, pltpu.roll(x, n - sh, 0), fill))
        else:
            x = op(x, jnp.where(row >= sh, pltpu.roll(x, sh, 0), fill))
        sh *= 2
    return x


def _mlstm_kernel(qf_ref, ktf_ref, vf_ref, gf_ref, qb_ref, ktb_ref, vb_ref, gb_ref, fb_ref, c0_ref, m0_ref,
                  hf_ref, hb_ref, c1_ref, m1_ref, c_s, m_s, *, nc):
    L, d = ML_CHUNK, ML_DH
    step = pl.program_id(1)

    @pl.when(step == 0)
    def _load_state():
        c_s[...] = c0_ref[0]
        m_s[...] = m0_ref[0]

    lane = lax.broadcasted_iota(jnp.int32, (L, LANES), 1)
    is_forget = (lane & ML_HEADS) != 0
    ti = lax.broadcasted_iota(jnp.int32, (L, L), 0)
    si = lax.broadcasted_iota(jnp.int32, (L, L), 1)
    ones = jnp.ones((L, d), BF16)
    dirs = ((False, qf_ref, ktf_ref, vf_ref, gf_ref, hf_ref), (True, qb_ref, ktb_ref, vb_ref, gb_ref, hb_ref))
    qk_all = [_dot(q_ref[0, :, d * hd:d * (hd + 1)], kt_ref[0, d * hd:d * (hd + 1), :])
              for (_, q_ref, kt_ref, _, _, _) in dirs for hd in range(ML_HEADS)]
    gate = []
    for di, (reverse, q_ref, kt_ref, v_ref, g_ref, h_ref) in enumerate(dirs):
        gates = g_ref[0]
        g2 = jnp.where(is_forget, _log_sigmoid(gates + fb_ref[...]), gates)
        a_cum = _scan_rows(jnp.where(is_forget, g2, 0.0), reverse, jnp.add, 0.0)
        b_t = pltpu.roll(g2, ML_HEADS, 1) - a_cum
        last = 0 if reverse else L - 1
        a_tot = a_cum[last:last + 1, :]
        m_old = m_s[di]
        mm = jnp.maximum(m_old, jnp.max(b_t, axis=0, keepdims=True))
        w_s_t = jnp.exp(b_t - mm).T
        decay = jnp.exp(m_old - mm)
        m_s[di] = a_tot + mm
        big_m = jnp.maximum(m_old, _scan_rows(b_t, reverse, jnp.maximum, MASK_VALUE))
        m_t = a_cum + big_m
        causal = (ti <= si) if reverse else (ti >= si)
        gate.append((m_old, w_s_t, decay, big_m, m_t, b_t.T, causal))

    results = []
    for di, (reverse, q_ref, kt_ref, v_ref, g_ref, h_ref) in enumerate(dirs):
        m_old, w_s_t, decay, big_m, m_t, b_rows, causal = gate[di]
        for hd in range(ML_HEADS):
            f = 2 * ML_HEADS * di + ML_HEADS + hd
            ch = ML_HEADS * di + hd
            q = q_ref[0, :, d * hd:d * (hd + 1)]
            kt = kt_ref[0, d * hd:d * (hd + 1), :]
            vext = jnp.concatenate([v_ref[0, :, d * hd:d * (hd + 1)], ones], axis=-1)
            c_old = c_s[ch]
            big_m_b = jnp.broadcast_to(big_m[:, f:f + 1], (L, L))
            m_old_b = jnp.broadcast_to(m_old[:, f:f + 1], (L, d))
            w_ts = jnp.exp(jnp.where(causal, b_rows[f:f + 1, :] - big_m_b, MASK_VALUE)) * qk_all[ch]
            w_c = jnp.exp(m_old_b - big_m_b)
            lhs = jnp.concatenate([w_ts.astype(BF16), (q.astype(F32) * w_c).astype(BF16)], axis=-1)
            rhs = jnp.concatenate([vext, c_old.astype(BF16)], axis=0)
            out = _dot(lhs, rhs)
            kw_t = (kt.astype(F32) * w_s_t[f:f + 1, :]).astype(BF16)
            m_t_b = jnp.broadcast_to(m_t[:, f:f + 1], (L, d))
            results.append((out, m_t_b, decay[:, f:f + 1] * c_old + _dot(kw_t, vext)))

    for di, (reverse, q_ref, kt_ref, v_ref, g_ref, h_ref) in enumerate(dirs):
        for hd in range(ML_HEADS):
            ch = ML_HEADS * di + hd
            out, m_t_b, c_new = results[ch]
            h_ref[0, :, d * hd:d * (hd + 1)] = out[:, 0:d] / jnp.maximum(jnp.abs(out[:, d:]), jnp.exp(-m_t_b))
            c_s[ch] = c_new

    @pl.when(step == nc - 1)
    def _store_state():
        c1_ref[0] = c_s[...]
        m1_ref[0] = m_s[...]


def _mlstm(q, kt, v, gates, fb_lane, state):
    bsz, n, _ = q.shape
    nc = n // ML_CHUNK
    L, w = ML_CHUNK, ML_WIDTH
    fwd = lambda b_, s: (b_, s, 0)
    bwd = lambda b_, s: (b_, nc - 1 - s, 0)
    fwd_t = lambda b_, s: (b_, 0, s)
    bwd_t = lambda b_, s: (b_, 0, nc - 1 - s)
    st = lambda b_, s: (b_, 0, 0, 0)
    c0, m0 = state
    c_shape, m_shape = (2 * ML_HEADS, ML_DH, 2 * ML_DH), (2, 1, LANES)
    tok = lambda width, im: pl.BlockSpec((1, L, width), im)
    out = pl.pallas_call(
        functools.partial(_mlstm_kernel, nc=nc),
        grid=(bsz, nc),
        in_specs=[tok(w, fwd), pl.BlockSpec((1, w, L), fwd_t), tok(w, fwd), tok(LANES, fwd),
                  tok(w, bwd), pl.BlockSpec((1, w, L), bwd_t), tok(w, bwd), tok(LANES, bwd),
                  pl.BlockSpec((1, LANES), lambda b_, s: (0, 0)),
                  pl.BlockSpec((1,) + c_shape, st), pl.BlockSpec((1,) + m_shape, st)],
        out_specs=[tok(w, fwd), tok(w, bwd), pl.BlockSpec((1,) + c_shape, st), pl.BlockSpec((1,) + m_shape, st)],
        out_shape=[jax.ShapeDtypeStruct((bsz, n, w), F32), jax.ShapeDtypeStruct((bsz, n, w), F32),
                   jax.ShapeDtypeStruct((bsz,) + c_shape, F32), jax.ShapeDtypeStruct((bsz,) + m_shape, F32)],
        scratch_shapes=[pltpu.VMEM(c_shape, F32), pltpu.VMEM(m_shape, F32)],
        compiler_params=_cparams(("parallel", "arbitrary")),
        name="mlstm",
    )(q, kt, v, gates, q, kt, v, gates, fb_lane, c0, m0)
    return out[0], out[1], (out[2], out[3])


def _na_kernel(*refs, n_blocks):
    if n_blocks:
        q_ref, kc_ref, vc_ref, kp_ref, k0_ref, k1_ref, vp_ref, v0_ref, v1_ref, bias_ref, o_ref = refs
        k_lat, v_lat = (kp_ref, k0_ref, k1_ref), (vp_ref, v0_ref, v1_ref)
    else:
        q_ref, kc_ref, vc_ref, o_ref = refs
        k_lat, v_lat = (), ()
    qb = q_ref.shape[2]
    row = lax.broadcasted_iota(jnp.int32, (LANES, qb), 0)

    def scores(hd):
        g, half = hd // 2, hd % 2
        q2 = q_ref[0, LANES * g:LANES * (g + 1), :]
        q = jnp.where((row >> 6) == half, q2, jnp.zeros_like(q2))
        sl = slice(LANES * g, LANES * (g + 1))
        parts = [_dot(kr[0, :, sl], q) for kr in k_lat]
        if n_blocks:
            bias = bias_ref[0, hd]
            parts = [p + bias[qb * t:qb * (t + 1), :] for t, p in enumerate(parts)]
        parts.append(_dot(kc_ref[0, :, sl], q))
        return parts

    def softmax(parts):
        mx = functools.reduce(jnp.maximum, [jnp.max(p, axis=0, keepdims=True) for p in parts])
        return [jnp.exp2(p - mx).astype(BF16) for p in parts]

    def weighted_values(hd, probs):
        g, half = hd // 2, hd % 2
        rows = slice(VN_ROWS * g, VN_ROWS * (g + 1))
        acc = None
        for p, vr in zip(probs, v_lat + (vc_ref,)):
            t = _dot(vr[0, rows, :], p)
            acc = t if acc is None else acc + t
        return acc[NA_DH * half:NA_DH * (half + 1), :] / acc[LANES:LANES + 1, :]

    all_scores = [scores(hd) for hd in range(NA_HEADS)]
    all_probs = [softmax(parts) for parts in all_scores]
    outs = [weighted_values(hd, probs) for hd, probs in enumerate(all_probs)]
    for g in range(NA_HEADS // 2):
        o_ref[0, :, LANES * g:LANES * (g + 1)] = jnp.concatenate(outs[2 * g:2 * g + 2], axis=0).T.astype(BF16)


def _na_attn(qnt, kn, vnt, kc, vct, bias, qb):
    bsz, w, n = qnt.shape
    t = kc.shape[1]
    n_blocks = 0 if bias is None else 3
    nb = n // qb
    vr = vct.shape[1]
    tok = lambda off: pl.BlockSpec((1, qb, w), lambda b_, i: (b_, jnp.clip(i + off, 0, nb - 1), 0))
    feat = lambda r, off: pl.BlockSpec((1, r, qb), lambda b_, i: (b_, 0, jnp.clip(i + off, 0, nb - 1)))
    ins = [qnt, kc, vct]
    specs = [feat(w, 0), pl.BlockSpec((1, t, w), lambda b_, i: (b_, 0, 0)), pl.BlockSpec((1, vr, t), lambda b_, i: (b_, 0, 0))]
    if n_blocks:
        ins += [kn, kn, kn, vnt, vnt, vnt, bias]
        block_class = lambda b_, i: (jnp.where(i == 0, 0, jnp.where(i == nb - 1, 2, 1)), 0, 0, 0)
        specs += [tok(-1), tok(0), tok(1), feat(vr, -1), feat(vr, 0), feat(vr, 1),
                  pl.BlockSpec((1,) + bias.shape[1:], block_class)]
    return pl.pallas_call(
        functools.partial(_na_kernel, n_blocks=n_blocks),
        grid=(bsz, nb),
        in_specs=specs,
        out_specs=pl.BlockSpec((1, qb, w), lambda b_, i: (b_, i, 0)),
        out_shape=jax.ShapeDtypeStruct((bsz, n, w), BF16),
        compiler_params=_cparams(("parallel", "parallel")),
        name="na_attn" if n_blocks else "ctx_attn",
    )(*ins)


def _even_cols(w):
    lead = w.shape[:-1]
    half, hr = DA_QK // 2, MLA_ROPE // 2
    qk = w[..., :1024].reshape(lead + (4, DA_HEADS, 2, half))
    qk = jnp.swapaxes(qk, -3, -2).reshape(lead + (1024,))
    kr = w[..., 1920:1952]
    pad = jnp.zeros(lead + (LANES - MLA_ROPE,), w.dtype)
    kr_swapped = jnp.concatenate([kr[..., hr:], kr[..., :hr]], axis=-1)
    return jnp.concatenate([qk, w[..., 1024:1920], kr, pad, kr_swapped, pad, w[..., 1952:]], axis=-1)


def _uq_cols(w):
    hr = MLA_ROPE // 2
    w3 = w.reshape(w.shape[0], MLA_HEADS, MLA_NOPE + MLA_ROPE)
    parts = [w3[..., :MLA_NOPE], w3[..., MLA_NOPE:MLA_NOPE + hr], w3[..., MLA_NOPE + hr:]]
    return jnp.concatenate([p.reshape(w.shape[0], -1) for p in parts], axis=-1)


def _ukv_cols(w):
    w3 = w.reshape(w.shape[0], MLA_HEADS, MLA_NOPE + MLA_V)
    return jnp.concatenate([w3[..., :MLA_NOPE].reshape(w.shape[0], -1), w3[..., MLA_NOPE:].reshape(w.shape[0], -1)], axis=-1)


def _odd_cols(w):
    pad = jnp.zeros(w.shape[:-1] + (LANES - 4 * ML_HEADS,), w.dtype)
    return jnp.concatenate([w[..., :2064], pad, w[..., 2064:]], axis=-1)


def _rope_tables(n_lat, n_ctx):
    t = jnp.arange(n_lat)
    row = (t // GRID_W).astype(F32)
    col = (t % GRID_W).astype(F32)

    def cs(dim):
        nf = dim // 4
        freqs = ROPE_BASE ** (-jnp.arange(nf, dtype=F32) / nf)
        ang = jnp.concatenate([row[:, None] * freqs, col[:, None] * freqs], axis=-1)
        return jnp.cos(ang), jnp.sin(ang)

    ca, sa = cs(DA_QK)
    cm, sm = cs(MLA_ROPE)
    lat = jnp.concatenate([jnp.tile(ca, (1, 4)), jnp.tile(sa, (1, 4)), jnp.tile(cm, (1, 8)), jnp.tile(sm, (1, 8)),
                           jnp.tile(jnp.concatenate([-sm, sm], axis=-1), (1, 4))], axis=-1)
    one, zero = jnp.ones((n_ctx, LANES), F32), jnp.zeros((n_ctx, LANES), F32)
    return lat, jnp.concatenate([one, zero, one, zero, zero], axis=-1)


def _na_bias_tiles(rpb, qb, rows):
    heads, n_dr, n_dc = rpb.shape
    rows_per = qb // GRID_W
    assert rows >= 3 * rows_per and NA_ROWS == 2 * rows_per
    kc, qc = np.arange(GRID_W)[:, None], np.arange(GRID_W)[None, :]
    c0 = np.clip(qc - NA_COLS // 2, 0, GRID_W - NA_COLS)
    col_ok = (kc >= c0) & (kc < c0 + NA_COLS)
    onehot = ((kc - qc + NA_COLS - 1)[None] == np.arange(n_dc)[:, None, None]) & col_ok[None]
    tiles = jnp.einsum('hrj,jp->hrp', rpb * LOG2E, jnp.asarray(onehot.reshape(n_dc, -1), F32),
                       precision=lax.Precision.HIGHEST).reshape(heads, n_dr, GRID_W, GRID_W)
    tiles = jnp.where(jnp.asarray(col_ok)[None, None], tiles, MASK_VALUE)
    kr = (np.arange(3 * rows_per) - rows_per)[:, None]
    qr = np.arange(rows_per)[None, :]
    dr = kr - qr
    idx = np.clip(dr + NA_ROWS - 1, 0, n_dr - 1).reshape(-1)
    blocks = jnp.take(tiles, jnp.asarray(idx), axis=1).reshape(heads, 3 * rows_per, rows_per, GRID_W, GRID_W)
    window = [(kr >= 0) & (kr < NA_ROWS) & (qr >= 0),
              (dr >= -(NA_ROWS // 2)) & (dr < NA_ROWS // 2),
              (kr >= rows_per - NA_ROWS) & (kr < rows_per) & (qr >= 0)]
    out = [jnp.where(jnp.asarray(ok)[None, :, :, None, None], blocks, MASK_VALUE) for ok in window]
    return jnp.stack(out).transpose(0, 1, 2, 4, 3, 5).reshape(3, heads, 3 * qb, qb)


def _tile_plan(n_lat, n_ctx):
    return {
        "tok_lat": 512,
        "tok_odd": 256,
        "tok_ctx": min(256, n_ctx),
        "attn_q": 1024 if n_lat % 1024 == 0 else 512,
        "attn_k": 1024 if n_lat % 1024 == 0 else 512,
        "na_q": 4 * GRID_W,
    }


def kernel(x, c, ctx, c_ctx, ada_w, ada_b, ln_g, ln_b, ev_w_in, ev_b_in, da_lambda, da_subln_g, mla_q_norm_g, mla_kv_norm_g, mla_w_uq, mla_w_ukv, ev_w_out, od_w_in, od_b_in, ml_conv_w, ml_conv_b, ml_f_bias, ml_norm_g, na_rpb, od_w_out):
    bsz, n_lat, d = x.shape
    n_ctx = ctx.shape[1]
    rows = n_lat // GRID_W
    assert d == D_MODEL and bsz + 1 <= 8 and n_lat % 512 == 0 and n_ctx % 256 == 0 and rows >= NA_ROWS

    cond8 = jnp.zeros((8, d), F32).at[:bsz].set(c).at[bsz].set(c_ctx)
    mod_all = _ada_mod(cond8, ada_w, ada_b)
    rope_lat, rope_ctx = _rope_tables(n_lat, n_ctx)

    tp = _tile_plan(n_lat, n_ctx)
    tm_lat, tm_ctx, tq, tk, na_qb = tp["tok_lat"], tp["tok_ctx"], tp["attn_q"], tp["attn_k"], tp["na_q"]
    xl, xc = x, ctx
    for l in range(DEPTH):
        update_ctx = l < DEPTH - 1
        i = l // 2
        mod_l = mod_all[l, :bsz].reshape(bsz, 1, 3 * d)
        mod_c = mod_all[l, bsz].reshape(1, 1, 3 * d)
        lng, lnb = ln_g[l].reshape(1, d), ln_b[l].reshape(1, d)
        if l % 2 == 0:
            lam_init = 0.8 - 0.6 * math.exp(-0.3 * l)
            w = _even_cols(ev_w_in[i]).astype(BF16)
            b = _even_cols(ev_b_in[i]).reshape(1, EV_COLS)
            wuq = _uq_cols(mla_w_uq[i]).astype(BF16)
            wukv = _ukv_cols(mla_w_ukv[i]).astype(BF16)
            qg = mla_q_norm_g[i].reshape(1, MLA_Q_RANK)
            kvg = mla_kv_norm_g[i].reshape(1, MLA_KV_RANK)
            subg = da_subln_g[i].reshape(DA_V, 1)
            wout = ev_w_out[i].astype(BF16)
            qt_l, ka_l, km_l, vt_l, g_l = _proj_even(xl, mod_l, rope_lat, w, b, qg, kvg, wuq, wukv, tm_lat)
            qt_c, ka_c, km_c, vt_c, g_c = _proj_even(xc, mod_c, rope_ctx, w, b, qg, kvg, wuq, wukv, tm_ctx)
            y_l = _attn_even(da_lambda[i], subg, qt_l, ka_l, km_l, vt_l, (ka_c, km_c, vt_c), lam_init, tq, tk)
            xl_new = _out_proj(_out_even_kernel, "out_even", (y_l, g_l), xl, mod_l, wout, (), lng, lnb, tm_lat)
            if update_ctx:
                y_c = _attn_even(da_lambda[i], subg, qt_c, ka_c, km_c, vt_c, None, lam_init, n_ctx, n_ctx)
                xc = _out_proj(_out_even_kernel, "out_even", (y_c, g_c), xc, mod_c, wout, (), lng, lnb, tm_ctx)
            xl = xl_new
        else:
            w = _odd_cols(od_w_in[i]).astype(BF16)
            b = _odd_cols(od_b_in[i]).reshape(1, OD_COLS)
            wout = od_w_out[i].astype(BF16)
            cw, cb = ml_conv_w[i], ml_conv_b[i].reshape(1, 2 * ML_WIDTH)
            fb = ml_f_bias[i].astype(F32)
            fb_lane = jnp.zeros((1, LANES), F32).at[0, 4:8].set(fb[0]).at[0, 12:16].set(fb[1])
            ng = ml_norm_g[i].reshape(1, ML_WIDTH)
            qk_l, v_l, so_l, gt_l, qn_l, kn_l, vn_l, g_l = _proj_odd(xl, mod_l, w, b, tp["tok_odd"])
            qk_c, v_c, so_c, gt_c, qn_c, kn_c, vn_c, g_c = _proj_odd(xc, mod_c, w, b, tm_ctx)
            q_l, kt_l = _conv_silu(qk_l, cw, cb, tp["tok_lat"])
            q_c, kt_c = _conv_silu(qk_c, cw, cb, n_ctx)
            zero = (jnp.zeros((bsz, 2 * ML_HEADS, ML_DH, 2 * ML_DH), F32), jnp.zeros((bsz, 2, 1, LANES), F32))
            hf_c, hb_c, st_c = _mlstm(q_c, kt_c, v_c, gt_c, fb_lane, zero)
            hf_l, hb_l, _ = _mlstm(q_l, kt_l, v_l, gt_l, fb_lane, st_c)
            bias = _na_bias_tiles(na_rpb[i].astype(F32), na_qb, rows)
            na_l = _na_attn(qn_l, kn_l, vn_l, kn_c, vn_c, bias, na_qb)
            xl_new = _out_proj(_out_odd_kernel, "out_odd", (hf_l, hb_l, so_l, na_l, g_l), xl, mod_l, wout, (ng,), lng, lnb, tm_lat)
            if update_ctx:
                na_c = _na_attn(qn_c, None, None, kn_c, vn_c, None, n_ctx)
                xc = _out_proj(_out_odd_kernel, "out_odd", (hf_c, hb_c, so_c, na_c, g_c), xc, mod_c, wout, (ng,), lng, lnb, tm_ctx)
            xl = xl_new
    return xl
```

```python
import functools
import math

import numpy as np
import jax
import jax.numpy as jnp
from jax import lax
from jax.experimental import pallas as pl
from jax.experimental.pallas import tpu as pltpu

F32 = jnp.float32
BF16 = jnp.bfloat16

D_MODEL = 1024
DEPTH = 4
GRID_W = 64
ROPE_BASE = 10000.0
LN_EPS = 1e-5
RMS_EPS = 1e-6
DEEPNORM_ALPHA = (2.0 * DEPTH) ** 0.25

DA_HEADS, DA_QK, DA_V = 4, 64, 128
LOG2E = math.log2(math.e)
DA_SCALE = DA_QK ** -0.5 * LOG2E
MLA_HEADS, MLA_Q_RANK, MLA_KV_RANK, MLA_NOPE, MLA_ROPE, MLA_V = 8, 256, 128, 64, 32, 64
MLA_SCALE = (MLA_NOPE + MLA_ROPE) ** -0.5 * LOG2E
ML_HEADS, ML_DH, ML_CONV, ML_CHUNK = 4, 128, 5, 128
ML_WIDTH = ML_HEADS * ML_DH
NA_HEADS, NA_DH, NA_ROWS, NA_COLS = 8, 64, 8, 16
NA_WIDTH = NA_HEADS * NA_DH
NA_SCALE = NA_DH ** -0.5 * LOG2E

LANES = 128
MASK_VALUE = -1e30
VMEM_LIMIT = 48 * 1024 * 1024

EV_COLS = 3200
EV_V, EV_CQ, EV_CKV, EV_KRA, EV_KRB, EV_GATE = 1024, 1536, 1792, 1920, 2048, 2176
OD_COLS = 4736
OD_V, OD_O, OD_GATES, OD_QN, OD_KN, OD_VN, OD_GATE = 1024, 1536, 2048, 2176, 2688, 3200, 3712
ONES_ROWS = 16
VA_ROWS = DA_V + ONES_ROWS
VM_ROWS = MLA_V + ONES_ROWS
VT_ROWS = DA_HEADS * VA_ROWS + MLA_HEADS * VM_ROWS
QT_ROWS = 1280
N_UNITS = 16
VN_ROWS = LANES + ONES_ROWS


def _cparams(sem):
    return pltpu.CompilerParams(dimension_semantics=sem, vmem_limit_bytes=VMEM_LIMIT)


def _silu(x):
    return x * jax.nn.sigmoid(x)


def _dot(a, b):
    return jnp.dot(a, b, preferred_element_type=F32)


def _log_sigmoid(x):
    return jnp.minimum(x, 0.0) - jnp.log(1.0 + jnp.exp(-jnp.abs(x)))


def _ada_kernel(c_ref, w_ref, b_ref, o_ref):
    c = c_ref[...]
    o_ref[0] = _dot(_silu(c).astype(BF16), w_ref[0].astype(BF16)) + b_ref[0]


def _ada_mod(cond8, ada_w, ada_b):
    depth, d, e = ada_w.shape
    tn = 1024
    return pl.pallas_call(
        _ada_kernel,
        grid=(depth, e // tn),
        in_specs=[pl.BlockSpec((8, d), lambda l, n: (0, 0)),
                  pl.BlockSpec((1, d, tn), lambda l, n: (l, 0, n)),
                  pl.BlockSpec((1, 1, tn), lambda l, n: (l, 0, n))],
        out_specs=pl.BlockSpec((1, 8, tn), lambda l, n: (l, 0, n)),
        out_shape=jax.ShapeDtypeStruct((depth, 8, e), F32),
        compiler_params=_cparams(("parallel", "parallel")),
        name="ada_mod",
    )(cond8, ada_w, ada_b.reshape(depth, 1, e))


def _mod_spec(mod):
    per_batch = mod.shape[0] > 1
    return pl.BlockSpec((1, 1, mod.shape[2]), (lambda b, i: (b, 0, 0)) if per_batch else (lambda b, i: (0, 0, 0)))


def _proj_even_kernel(x_ref, mod_ref, rope_ref, w_ref, b_ref, qg_ref, kvg_ref, wuq_ref, wukv_ref,
                      qt_ref, ka_ref, km_ref, vt_ref, g_ref):
    d = D_MODEL
    x = x_ref[0]
    mod = mod_ref[0]
    h = (x * (1.0 + mod[:, d:2 * d]) + mod[:, 0:d]).astype(BF16)
    z = _dot(h, w_ref[...]) + b_ref[...]
    rope = rope_ref[...]
    cos_a, sin_a, cos_m, sin_m, sin_ms = (rope[:, LANES * i:LANES * (i + 1)] for i in range(5))

    def blk(o):
        return z[:, o:o + LANES]

    def rot(a, b, c, s):
        return a * c - b * s, b * c + a * s

    q1a, q1b = rot(blk(0), blk(128), cos_a, sin_a)
    q2a, q2b = rot(blk(256), blk(384), cos_a, sin_a)
    k1a, k1b = rot(blk(512), blk(640), cos_a, sin_a)
    k2a, k2b = rot(blk(768), blk(896), cos_a, sin_a)
    for i, t in enumerate((q1a, q1b, q2a, q2b)):
        qt_ref[0, LANES * i:LANES * (i + 1), :] = (t * DA_SCALE).T.astype(BF16)
    for i, t in enumerate((k1a, k1b, k2a, k2b)):
        ka_ref[0, i // 2, :, LANES * (i % 2):LANES * (i % 2 + 1)] = t.astype(BF16)
    for i in range(4):
        vt_ref[0, VA_ROWS * i:VA_ROWS * i + DA_V, :] = blk(EV_V + LANES * i).T.astype(BF16)
        vt_ref[0, VA_ROWS * i + DA_V:VA_ROWS * (i + 1), :] = jnp.ones((ONES_ROWS, x.shape[0]), BF16)

    def rms(t, g):
        return (t * lax.rsqrt(jnp.mean(t * t, axis=-1, keepdims=True) + RMS_EPS) * g).astype(BF16)

    qm = _dot(rms(z[:, EV_CQ:EV_CQ + MLA_Q_RANK], qg_ref[...]), wuq_ref[...])
    qra, qrb = rot(qm[:, 512:640], qm[:, 640:768], cos_m, sin_m)
    for i in range(4):
        qt_ref[0, 512 + LANES * i:512 + LANES * (i + 1), :] = (qm[:, LANES * i:LANES * (i + 1)] * MLA_SCALE).T.astype(BF16)
    qt_ref[0, 1024:1152, :] = (qra * MLA_SCALE).T.astype(BF16)
    qt_ref[0, 1152:1280, :] = (qrb * MLA_SCALE).T.astype(BF16)

    kvm = _dot(rms(z[:, EV_CKV:EV_CKV + MLA_KV_RANK], kvg_ref[...]), wukv_ref[...])
    k_rope = (blk(EV_KRA) * cos_m + blk(EV_KRB) * sin_ms).astype(BF16)
    for g in range(4):
        km_ref[0, g, :, 0:LANES] = kvm[:, LANES * g:LANES * (g + 1)].astype(BF16)
        km_ref[0, g, :, LANES:2 * LANES] = k_rope
        vm_t = kvm[:, 512 + LANES * g:512 + LANES * (g + 1)].T.astype(BF16)
        for half in range(2):
            r0 = DA_HEADS * VA_ROWS + VM_ROWS * (2 * g + half)
            vt_ref[0, r0:r0 + MLA_V, :] = vm_t[MLA_V * half:MLA_V * (half + 1), :]
            vt_ref[0, r0 + MLA_V:r0 + VM_ROWS, :] = jnp.ones((ONES_ROWS, x.shape[0]), BF16)

    g_ref[0] = _silu(z[:, EV_GATE:EV_GATE + d]).astype(BF16)


def _proj_even(x, mod, rope, w, b, qg, kvg, wuq, wukv, tm):
    bsz, n, d = x.shape
    const = lambda shape: pl.BlockSpec(shape, lambda b_, i: (0,) * len(shape))
    return pl.pallas_call(
        _proj_even_kernel,
        grid=(bsz, n // tm),
        in_specs=[pl.BlockSpec((1, tm, d), lambda b_, i: (b_, i, 0)),
                  _mod_spec(mod),
                  pl.BlockSpec((tm, 5 * LANES), lambda b_, i: (i, 0)),
                  const((d, EV_COLS)), const((1, EV_COLS)), const((1, MLA_Q_RANK)), const((1, MLA_KV_RANK)),
                  const((MLA_Q_RANK, 768)), const((MLA_KV_RANK, 1024))],
        out_specs=[pl.BlockSpec((1, QT_ROWS, tm), lambda b_, i: (b_, 0, i)),
                   pl.BlockSpec((1, 2, tm, 256), lambda b_, i: (b_, 0, i, 0)),
                   pl.BlockSpec((1, 4, tm, 256), lambda b_, i: (b_, 0, i, 0)),
                   pl.BlockSpec((1, VT_ROWS, tm), lambda b_, i: (b_, 0, i)),
                   pl.BlockSpec((1, tm, d), lambda b_, i: (b_, i, 0))],
        out_shape=[jax.ShapeDtypeStruct((bsz, QT_ROWS, n), BF16),
                   jax.ShapeDtypeStruct((bsz, 2, n, 256), BF16),
                   jax.ShapeDtypeStruct((bsz, 4, n, 256), BF16),
                   jax.ShapeDtypeStruct((bsz, VT_ROWS, n), BF16),
                   jax.ShapeDtypeStruct((bsz, n, d), BF16)],
        compiler_params=_cparams(("parallel", "parallel")),
        name="proj_even",
    )(x, mod, rope, w, b, qg, kvg, wuq, wukv)


def _attn_even_kernel(*refs, tq, nkv, sub_tk, has_ctx, lam_init):
    if has_ctx:
        (lam_ref, subg_ref, qt_ref, ka_ref, km_ref, vt_ref, kac_ref, kmc_ref, vtc_ref,
         y_ref, qw, acc_a, acc_m, m_s) = refs
    else:
        (lam_ref, subg_ref, qt_ref, ka_ref, km_ref, vt_ref,
         y_ref, qw, acc_a, acc_m, m_s) = refs
    j = pl.program_id(2)

    def process(ka, km, vt, n_sub, sub):
        def scores(t, u):
            rows = slice(sub * t, sub * (t + 1))
            k = ka[0, u // DA_HEADS, rows, :] if u < 2 * DA_HEADS else km[0, (u - 2 * DA_HEADS) // 2, rows, :]
            return _dot(k, qw[u])

        def softmax_step(u, s):
            m_old = m_s[u]
            m_new = jnp.maximum(m_old, jnp.max(s, axis=0, keepdims=True))
            alpha = jnp.exp2(m_old - m_new)
            p = jnp.exp2(s - m_new)
            m_s[u] = m_new
            return alpha, p.astype(BF16)

        def pv(t, u, alpha, p):
            cols = slice(sub * t, sub * (t + 1))
            if u < 2 * DA_HEADS:
                hd = u % DA_HEADS
                acc_a[u] = alpha * acc_a[u] + _dot(vt[0, VA_ROWS * hd:VA_ROWS * (hd + 1), cols], p)
            else:
                hd = u - 2 * DA_HEADS
                v = vt[0, DA_HEADS * VA_ROWS + VM_ROWS * hd:DA_HEADS * VA_ROWS + VM_ROWS * (hd + 1), cols]
                acc_m[hd] = alpha * acc_m[hd] + _dot(v, p)

        pairs = [(t, u) for t in range(n_sub) for u in range(N_UNITS)]
        s_next = scores(*pairs[0])
        pending = None
        for i, (t, u) in enumerate(pairs):
            s_cur = s_next
            if i + 1 < len(pairs):
                s_next = scores(*pairs[i + 1])
            if pending is not None:
                pv(*pending)
            alpha, p = softmax_step(u, s_cur)
            pending = (t, u, alpha, p)
        pv(*pending)

    @pl.when(j == 0)
    def _init():
        row = lax.broadcasted_iota(jnp.int32, (256, tq), 0)
        head_of_row = (row >> 5) & (DA_HEADS - 1)
        for mp in range(2):
            q = qt_ref[0, 256 * mp:256 * (mp + 1), :]
            for hd in range(DA_HEADS):
                qw[mp * DA_HEADS + hd] = jnp.where(head_of_row == hd, q, jnp.zeros_like(q))
        for hd in range(MLA_HEADS):
            u = 2 * DA_HEADS + hd
            qw[u] = jnp.zeros((256, tq), BF16)
            qw[u, 64 * (hd % 2):64 * (hd % 2) + 64, :] = qt_ref[0, 512 + 64 * hd:512 + 64 * hd + 64, :]
            qw[u, 128:144, :] = qt_ref[0, 1024 + 16 * hd:1024 + 16 * hd + 16, :]
            qw[u, 144:160, :] = qt_ref[0, 1152 + 16 * hd:1152 + 16 * hd + 16, :]
        acc_a[...] = jnp.zeros(acc_a.shape, F32)
        acc_m[...] = jnp.zeros(acc_m.shape, F32)
        m_s[...] = jnp.full(m_s.shape, MASK_VALUE, F32)
        if has_ctx:
            process(kac_ref, kmc_ref, vtc_ref, 1, kac_ref.shape[2])

    process(ka_ref, km_ref, vt_ref, ka_ref.shape[2] // sub_tk, sub_tk)

    @pl.when(j == nkv - 1)
    def _finish():
        lm = lam_ref[...]
        lam = (jnp.exp(jnp.sum(lm[0:1] * lm[1:2], axis=-1, keepdims=True))
               - jnp.exp(jnp.sum(lm[2:3] * lm[3:4], axis=-1, keepdims=True)) + lam_init)
        for hd in range(DA_HEADS):
            a1, a2 = acc_a[hd], acc_a[DA_HEADS + hd]
            o = a1[0:DA_V] / a1[DA_V:DA_V + 1] - lam * (a2[0:DA_V] / a2[DA_V:DA_V + 1])
            o = o * lax.rsqrt(jnp.mean(o * o, axis=0, keepdims=True) + RMS_EPS) * subg_ref[...] * (1.0 - lam_init)
            y_ref[0, :, DA_V * hd:DA_V * (hd + 1)] = o.T.astype(BF16)
        for g in range(MLA_HEADS // 2):
            b1, b2 = acc_m[2 * g], acc_m[2 * g + 1]
            o = jnp.concatenate([b1[0:MLA_V] / b1[MLA_V:MLA_V + 1], b2[0:MLA_V] / b2[MLA_V:MLA_V + 1]], axis=0)
            y_ref[0, :, 512 + LANES * g:512 + LANES * (g + 1)] = o.T.astype(BF16)


def _attn_even(lam_p, subg, qt, ka, km, vt, ctx_kv, lam_init, tq, tk):
    bsz, _, n = qt.shape
    nk = ka.shape[2]
    nq, nkv = n // tq, nk // tk
    has_ctx = ctx_kv is not None
    in_specs = [pl.BlockSpec((4, DA_QK), lambda b, i, j: (0, 0)),
                pl.BlockSpec((DA_V, 1), lambda b, i, j: (0, 0)),
                pl.BlockSpec((1, QT_ROWS, tq), lambda b, i, j: (b, 0, i)),
                pl.BlockSpec((1, 2, tk, 256), lambda b, i, j: (b, 0, j, 0)),
                pl.BlockSpec((1, 4, tk, 256), lambda b, i, j: (b, 0, j, 0)),
                pl.BlockSpec((1, VT_ROWS, tk), lambda b, i, j: (b, 0, j))]
    args = [lam_p, subg, qt, ka, km, vt]
    if has_ctx:
        tc = ctx_kv[0].shape[2]
        in_specs += [pl.BlockSpec((1, 2, tc, 256), lambda b, i, j: (b, 0, 0, 0)),
                     pl.BlockSpec((1, 4, tc, 256), lambda b, i, j: (b, 0, 0, 0)),
                     pl.BlockSpec((1, VT_ROWS, tc), lambda b, i, j: (b, 0, 0))]
        args += list(ctx_kv)
    return pl.pallas_call(
        functools.partial(_attn_even_kernel, tq=tq, nkv=nkv, sub_tk=min(tk, 512), has_ctx=has_ctx, lam_init=lam_init),
        grid=(bsz, nq, nkv),
        in_specs=in_specs,
        out_specs=pl.BlockSpec((1, tq, D_MODEL), lambda b, i, j: (b, i, 0)),
        out_shape=jax.ShapeDtypeStruct((bsz, n, D_MODEL), BF16),
        scratch_shapes=[pltpu.VMEM((N_UNITS, 256, tq), BF16),
                        pltpu.VMEM((2 * DA_HEADS, VA_ROWS, tq), F32),
                        pltpu.VMEM((MLA_HEADS, VM_ROWS, tq), F32),
                        pltpu.VMEM((N_UNITS, 1, tq), F32)],
        compiler_params=_cparams(("parallel", "parallel", "arbitrary")),
        name="attn_even",
    )(*args)


def _layer_norm_rows(t, g, b):
    mu = jnp.mean(t, axis=-1, keepdims=True)
    tc = t - mu
    var = jnp.mean(tc * tc, axis=-1, keepdims=True)
    return tc * lax.rsqrt(var + LN_EPS) * g + b


def _out_even_kernel(y_ref, g_ref, x_ref, mod_ref, w_ref, lng_ref, lnb_ref, o_ref):
    d = D_MODEL
    y = (y_ref[0].astype(F32) * g_ref[0].astype(F32)).astype(BF16)
    out = _dot(y, w_ref[...])
    t = DEEPNORM_ALPHA * x_ref[0] + mod_ref[0][:, 2 * d:3 * d] * out
    o_ref[0] = _layer_norm_rows(t, lng_ref[...], lnb_ref[...])


def _out_odd_kernel(hf_ref, hb_ref, so_ref, na_ref, g_ref, x_ref, mod_ref, w_ref, ng_ref, lng_ref, lnb_ref, o_ref):
    d = D_MODEL
    hsum = hf_ref[0].astype(F32) + hb_ref[0].astype(F32)
    parts = []
    for hd in range(ML_HEADS):
        t = hsum[:, ML_DH * hd:ML_DH * (hd + 1)]
        mu = jnp.mean(t, axis=-1, keepdims=True)
        tc = t - mu
        var = jnp.mean(tc * tc, axis=-1, keepdims=True)
        parts.append(tc * lax.rsqrt(var + LN_EPS))
    y_ml = jnp.concatenate(parts, axis=-1) * ng_ref[...] * so_ref[0].astype(F32)
    g = g_ref[0].astype(F32)
    y = jnp.concatenate([y_ml * g[:, 0:ML_WIDTH], na_ref[0].astype(F32) * g[:, ML_WIDTH:]], axis=-1).astype(BF16)
    out = _dot(y, w_ref[...])
    t = DEEPNORM_ALPHA * x_ref[0] + mod_ref[0][:, 2 * d:3 * d] * out
    o_ref[0] = _layer_norm_rows(t, lng_ref[...], lnb_ref[...])


def _out_proj(kernel_fn, name, acts, x, mod, w, extra, ln_g, ln_b, tm):
    bsz, n, d = x.shape
    tok = lambda a: pl.BlockSpec((1, tm, a.shape[2]), lambda b_, i: (b_, i, 0))
    const = lambda a: pl.BlockSpec(a.shape, lambda b_, i: (0,) * a.ndim)
    ins = list(acts) + [x, mod, w] + list(extra) + [ln_g, ln_b]
    specs = [tok(a) for a in acts] + [tok(x), _mod_spec(mod), const(w)] + [const(e) for e in extra] + [const(ln_g), const(ln_b)]
    return pl.pallas_call(
        kernel_fn,
        grid=(bsz, n // tm),
        in_specs=specs,
        out_specs=pl.BlockSpec((1, tm, d), lambda b_, i: (b_, i, 0)),
        out_shape=jax.ShapeDtypeStruct((bsz, n, d), F32),
        compiler_params=_cparams(("parallel", "parallel")),
        name=name,
    )(*ins)


def _proj_odd_kernel(x_ref, mod_ref, w_ref, b_ref, qk_ref, v_ref, so_ref, gt_ref, qn_ref, kn_ref, vn_ref, g_ref):
    d = D_MODEL
    x = x_ref[0]
    mod = mod_ref[0]
    h = (x * (1.0 + mod[:, d:2 * d]) + mod[:, 0:d]).astype(BF16)
    z = _dot(h, w_ref[...]) + b_ref[...]
    qk_ref[0] = z[:, 0:2 * ML_WIDTH]
    v_ref[0] = z[:, OD_V:OD_V + ML_WIDTH].astype(BF16)
    so_ref[0] = jax.nn.sigmoid(z[:, OD_O:OD_O + ML_WIDTH]).astype(BF16)
    gt_ref[0] = z[:, OD_GATES:OD_GATES + LANES]
    kn_ref[0] = z[:, OD_KN:OD_KN + NA_WIDTH].astype(BF16)
    for g in range(NA_HEADS // 2):
        qn_ref[0, LANES * g:LANES * (g + 1), :] = (z[:, OD_QN + LANES * g:OD_QN + LANES * (g + 1)] * NA_SCALE).T.astype(BF16)
        vn_ref[0, VN_ROWS * g:VN_ROWS * g + LANES, :] = z[:, OD_VN + LANES * g:OD_VN + LANES * (g + 1)].T.astype(BF16)
        vn_ref[0, VN_ROWS * g + LANES:VN_ROWS * (g + 1), :] = jnp.ones((ONES_ROWS, x.shape[0]), BF16)
    g_ref[0] = _silu(z[:, OD_GATE:OD_GATE + d]).astype(BF16)


def _proj_odd(x, mod, w, b, tm):
    bsz, n, d = x.shape
    const = lambda shape: pl.BlockSpec(shape, lambda b_, i: (0,) * len(shape))
    tok = lambda wd, dt: (pl.BlockSpec((1, tm, wd), lambda b_, i: (b_, i, 0)), jax.ShapeDtypeStruct((bsz, n, wd), dt))
    feat = lambda r, dt: (pl.BlockSpec((1, r, tm), lambda b_, i: (b_, 0, i)), jax.ShapeDtypeStruct((bsz, r, n), dt))
    outs = [tok(2 * ML_WIDTH, F32), tok(ML_WIDTH, BF16), tok(ML_WIDTH, BF16), tok(LANES, F32),
            feat(NA_WIDTH, BF16), tok(NA_WIDTH, BF16), feat(VN_ROWS * NA_HEADS // 2, BF16), tok(d, BF16)]
    return pl.pallas_call(
        _proj_odd_kernel,
        grid=(bsz, n // tm),
        in_specs=[pl.BlockSpec((1, tm, d), lambda b_, i: (b_, i, 0)), _mod_spec(mod),
                  const((d, OD_COLS)), const((1, OD_COLS))],
        out_specs=[o[0] for o in outs],
        out_shape=[o[1] for o in outs],
        compiler_params=_cparams(("parallel", "parallel")),
        name="proj_odd",
    )(x, mod, w, b)


def _conv_kernel(cur_ref, prev_ref, next_ref, w_ref, b_ref, q_ref, kt_ref, buf, *, tm, nt):
    i = pl.program_id(1)
    buf[0:8, :] = jnp.where(i > 0, prev_ref[0], 0.0)
    buf[8:8 + tm, :] = cur_ref[0]
    buf[8 + tm:16 + tm, :] = jnp.where(i < nt - 1, next_ref[0], 0.0)
    w = w_ref[...]
    acc = b_ref[...] + buf[6:6 + tm, :] * w[0:1]
    for tap in range(1, ML_CONV):
        acc = acc + buf[6 + tap:6 + tap + tm, :] * w[tap:tap + 1]
    y = _silu(acc)
    q_ref[0] = y[:, 0:ML_WIDTH].astype(BF16)
    for hd in range(ML_HEADS):
        k = y[:, ML_WIDTH + ML_DH * hd:ML_WIDTH + ML_DH * (hd + 1)] * (ML_DH ** -0.5)
        kt_ref[0, ML_DH * hd:ML_DH * (hd + 1), :] = k.T.astype(BF16)


def _conv_silu(qk, w, b, tm):
    bsz, n, c = qk.shape
    nt = n // tm
    r = tm // 8
    last8 = n // 8 - 1
    return pl.pallas_call(
        functools.partial(_conv_kernel, tm=tm, nt=nt),
        grid=(bsz, nt),
        in_specs=[pl.BlockSpec((1, tm, c), lambda b_, i: (b_, i, 0)),
                  pl.BlockSpec((1, 8, c), lambda b_, i: (b_, jnp.maximum(i * r - 1, 0), 0)),
                  pl.BlockSpec((1, 8, c), lambda b_, i: (b_, jnp.minimum((i + 1) * r, last8), 0)),
                  pl.BlockSpec((ML_CONV, c), lambda b_, i: (0, 0)),
                  pl.BlockSpec((1, c), lambda b_, i: (0, 0))],
        out_specs=[pl.BlockSpec((1, tm, ML_WIDTH), lambda b_, i: (b_, i, 0)),
                   pl.BlockSpec((1, ML_WIDTH, tm), lambda b_, i: (b_, 0, i))],
        out_shape=[jax.ShapeDtypeStruct((bsz, n, ML_WIDTH), BF16),
                   jax.ShapeDtypeStruct((bsz, ML_WIDTH, n), BF16)],
        scratch_shapes=[pltpu.VMEM((tm + 16, c), F32)],
        compiler_params=_cparams(("parallel", "parallel")),
        name="conv_silu",
    )(qk, qk, qk, w, b)


def _scan_rows(x, reverse, op, fill):
    n = x.shape[0]
    row = lax.broadcasted_iota(jnp.int32, x.shape, 0)
    sh = 1
    while sh < n:
        if reverse:
            x = op(x, jnp.where(row + sh < n, pltpu.roll(x, n - sh, 0), fill))
        else:
            x = op(x, jnp.where(row >= sh, pltpu.roll(x, sh, 0), fill))
        sh *= 2
    return x


def _mlstm_kernel(qf_ref, ktf_ref, vf_ref, gf_ref, qb_ref, ktb_ref, vb_ref, gb_ref, fb_ref, c0_ref, m0_ref,
                  hf_ref, hb_ref, c1_ref, m1_ref, c_s, m_s, *, nc):
    L, d = ML_CHUNK, ML_DH
    step = pl.program_id(1)

    @pl.when(step == 0)
    def _load_state():
        c_s[...] = c0_ref[0]
        m_s[...] = m0_ref[0]

    lane = lax.broadcasted_iota(jnp.int32, (L, LANES), 1)
    is_forget = (lane & ML_HEADS) != 0
    ti = lax.broadcasted_iota(jnp.int32, (L, L), 0)
    si = lax.broadcasted_iota(jnp.int32, (L, L), 1)
    ones = jnp.ones((L, d), BF16)
    dirs = ((False, qf_ref, ktf_ref, vf_ref, gf_ref, hf_ref), (True, qb_ref, ktb_ref, vb_ref, gb_ref, hb_ref))
    qk_all = [_dot(q_ref[0, :, d * hd:d * (hd + 1)], kt_ref[0, d * hd:d * (hd + 1), :])
              for (_, q_ref, kt_ref, _, _, _) in dirs for hd in range(ML_HEADS)]
    gate = []
    for di, (reverse, q_ref, kt_ref, v_ref, g_ref, h_ref) in enumerate(dirs):
        gates = g_ref[0]
        g2 = jnp.where(is_forget, _log_sigmoid(gates + fb_ref[...]), gates)
        a_cum = _scan_rows(jnp.where(is_forget, g2, 0.0), reverse, jnp.add, 0.0)
        b_t = pltpu.roll(g2, ML_HEADS, 1) - a_cum
        last = 0 if reverse else L - 1
        a_tot = a_cum[last:last + 1, :]
        m_old = m_s[di]
        mm = jnp.maximum(m_old, jnp.max(b_t, axis=0, keepdims=True))
        w_s_t = jnp.exp(b_t - mm).T
        decay = jnp.exp(m_old - mm)
        m_s[di] = a_tot + mm
        big_m = jnp.maximum(m_old, _scan_rows(b_t, reverse, jnp.maximum, MASK_VALUE))
        m_t = a_cum + big_m
        causal = (ti <= si) if reverse else (ti >= si)
        gate.append((m_old, w_s_t, decay, big_m, m_t, b_t.T, causal))

    results = []
    for di, (reverse, q_ref, kt_ref, v_ref, g_ref, h_ref) in enumerate(dirs):
        m_old, w_s_t, decay, big_m, m_t, b_rows, causal = gate[di]
        for hd in range(ML_HEADS):
            f = 2 * ML_HEADS * di + ML_HEADS + hd
            ch = ML_HEADS * di + hd
            q = q_ref[0, :, d * hd:d * (hd + 1)]
            kt = kt_ref[0, d * hd:d * (hd + 1), :]
            vext = jnp.concatenate([v_ref[0, :, d * hd:d * (hd + 1)], ones], axis=-1)
            c_old = c_s[ch]
            big_m_b = jnp.broadcast_to(big_m[:, f:f + 1], (L, L))
            m_old_b = jnp.broadcast_to(m_old[:, f:f + 1], (L, d))
            w_ts = jnp.exp(jnp.where(causal, b_rows[f:f + 1, :] - big_m_b, MASK_VALUE)) * qk_all[ch]
            w_c = jnp.exp(m_old_b - big_m_b)
            lhs = jnp.concatenate([w_ts.astype(BF16), (q.astype(F32) * w_c).astype(BF16)], axis=-1)
            rhs = jnp.concatenate([vext, c_old.astype(BF16)], axis=0)
            out = _dot(lhs, rhs)
            kw_t = (kt.astype(F32) * w_s_t[f:f + 1, :]).astype(BF16)
            m_t_b = jnp.broadcast_to(m_t[:, f:f + 1], (L, d))
            results.append((out, m_t_b, decay[:, f:f + 1] * c_old + _dot(kw_t, vext)))

    for di, (reverse, q_ref, kt_ref, v_ref, g_ref, h_ref) in enumerate(dirs):
        for hd in range(ML_HEADS):
            ch = ML_HEADS * di + hd
            out, m_t_b, c_new = results[ch]
            h = out[:, 0:d] / jnp.maximum(jnp.abs(out[:, d:]), jnp.exp(-m_t_b))
            h_ref[0, :, d * hd:d * (hd + 1)] = h.astype(BF16)
            c_s[ch] = c_new

    @pl.when(step == nc - 1)
    def _store_state():
        c1_ref[0] = c_s[...]
        m1_ref[0] = m_s[...]


def _mlstm(q, kt, v, gates, fb_lane, state):
    bsz, n, _ = q.shape
    nc = n // ML_CHUNK
    L, w = ML_CHUNK, ML_WIDTH
    fwd = lambda b_, s: (b_, s, 0)
    bwd = lambda b_, s: (b_, nc - 1 - s, 0)
    fwd_t = lambda b_, s: (b_, 0, s)
    bwd_t = lambda b_, s: (b_, 0, nc - 1 - s)
    st = lambda b_, s: (b_, 0, 0, 0)
    c0, m0 = state
    c_shape, m_shape = (2 * ML_HEADS, ML_DH, 2 * ML_DH), (2, 1, LANES)
    tok = lambda width, im: pl.BlockSpec((1, L, width), im)
    out = pl.pallas_call(
        functools.partial(_mlstm_kernel, nc=nc),
        grid=(bsz, nc),
        in_specs=[tok(w, fwd), pl.BlockSpec((1, w, L), fwd_t), tok(w, fwd), tok(LANES, fwd),
                  tok(w, bwd), pl.BlockSpec((1, w, L), bwd_t), tok(w, bwd), tok(LANES, bwd),
                  pl.BlockSpec((1, LANES), lambda b_, s: (0, 0)),
                  pl.BlockSpec((1,) + c_shape, st), pl.BlockSpec((1,) + m_shape, st)],
        out_specs=[tok(w, fwd), tok(w, bwd), pl.BlockSpec((1,) + c_shape, st), pl.BlockSpec((1,) + m_shape, st)],
        out_shape=[jax.ShapeDtypeStruct((bsz, n, w), BF16), jax.ShapeDtypeStruct((bsz, n, w), BF16),
                   jax.ShapeDtypeStruct((bsz,) + c_shape, F32), jax.ShapeDtypeStruct((bsz,) + m_shape, F32)],
        scratch_shapes=[pltpu.VMEM(c_shape, F32), pltpu.VMEM(m_shape, F32)],
        compiler_params=_cparams(("parallel", "arbitrary")),
        name="mlstm",
    )(q, kt, v, gates, q, kt, v, gates, fb_lane, c0, m0)
    return out[0], out[1], (out[2], out[3])


def _na_kernel(*refs, n_blocks):
    if n_blocks:
        q_ref, kc_ref, vc_ref, kp_ref, k0_ref, k1_ref, vp_ref, v0_ref, v1_ref, bias_ref, o_ref = refs
        k_lat, v_lat = (kp_ref, k0_ref, k1_ref), (vp_ref, v0_ref, v1_ref)
    else:
        q_ref, kc_ref, vc_ref, o_ref = refs
        k_lat, v_lat = (), ()
    qb = q_ref.shape[2]
    row = lax.broadcasted_iota(jnp.int32, (LANES, qb), 0)

    def scores(hd):
        g, half = hd // 2, hd % 2
        q2 = q_ref[0, LANES * g:LANES * (g + 1), :]
        q = jnp.where((row >> 6) == half, q2, jnp.zeros_like(q2))
        sl = slice(LANES * g, LANES * (g + 1))
        parts = [_dot(kr[0, :, sl], q) for kr in k_lat]
        if n_blocks:
            bias = bias_ref[0, hd]
            parts = [p + bias[qb * t:qb * (t + 1), :] for t, p in enumerate(parts)]
        parts.append(_dot(kc_ref[0, :, sl], q))
        return parts

    def softmax(parts):
        mx = functools.reduce(jnp.maximum, [jnp.max(p, axis=0, keepdims=True) for p in parts])
        return [jnp.exp2(p - mx).astype(BF16) for p in parts]

    def weighted_values(hd, probs):
        g, half = hd // 2, hd % 2
        rows = slice(VN_ROWS * g, VN_ROWS * (g + 1))
        acc = None
        for p, vr in zip(probs, v_lat + (vc_ref,)):
            t = _dot(vr[0, rows, :], p)
            acc = t if acc is None else acc + t
        return acc[NA_DH * half:NA_DH * (half + 1), :] / acc[LANES:LANES + 1, :]

    all_scores = [scores(hd) for hd in range(NA_HEADS)]
    all_probs = [softmax(parts) for parts in all_scores]
    outs = [weighted_values(hd, probs) for hd, probs in enumerate(all_probs)]
    for g in range(NA_HEADS // 2):
        o_ref[0, :, LANES * g:LANES * (g + 1)] = jnp.concatenate(outs[2 * g:2 * g + 2], axis=0).T.astype(BF16)


def _na_attn(qnt, kn, vnt, kc, vct, bias, qb):
    bsz, w, n = qnt.shape
    t = kc.shape[1]
    n_blocks = 0 if bias is None else 3
    nb = n // qb
    vr = vct.shape[1]
    tok = lambda off: pl.BlockSpec((1, qb, w), lambda b_, i: (b_, jnp.clip(i + off, 0, nb - 1), 0))
    feat = lambda r, off: pl.BlockSpec((1, r, qb), lambda b_, i: (b_, 0, jnp.clip(i + off, 0, nb - 1)))
    ins = [qnt, kc, vct]
    specs = [feat(w, 0), pl.BlockSpec((1, t, w), lambda b_, i: (b_, 0, 0)), pl.BlockSpec((1, vr, t), lambda b_, i: (b_, 0, 0))]
    if n_blocks:
        ins += [kn, kn, kn, vnt, vnt, vnt, bias]
        block_class = lambda b_, i: (jnp.where(i == 0, 0, jnp.where(i == nb - 1, 2, 1)), 0, 0, 0)
        specs += [tok(-1), tok(0), tok(1), feat(vr, -1), feat(vr, 0), feat(vr, 1),
                  pl.BlockSpec((1,) + bias.shape[1:], block_class)]
    return pl.pallas_call(
        functools.partial(_na_kernel, n_blocks=n_blocks),
        grid=(bsz, nb),
        in_specs=specs,
        out_specs=pl.BlockSpec((1, qb, w), lambda b_, i: (b_, i, 0)),
        out_shape=jax.ShapeDtypeStruct((bsz, n, w), BF16),
        compiler_params=_cparams(("parallel", "parallel")),
        name="na_attn" if n_blocks else "ctx_attn",
    )(*ins)


def _even_cols(w):
    lead = w.shape[:-1]
    half, hr = DA_QK // 2, MLA_ROPE // 2
    qk = w[..., :1024].reshape(lead + (4, DA_HEADS, 2, half))
    qk = jnp.swapaxes(qk, -3, -2).reshape(lead + (1024,))
    kr = w[..., 1920:1952]
    pad = jnp.zeros(lead + (LANES - MLA_ROPE,), w.dtype)
    kr_swapped = jnp.concatenate([kr[..., hr:], kr[..., :hr]], axis=-1)
    return jnp.concatenate([qk, w[..., 1024:1920], kr, pad, kr_swapped, pad, w[..., 1952:]], axis=-1)


def _uq_cols(w):
    hr = MLA_ROPE // 2
    w3 = w.reshape(w.shape[0], MLA_HEADS, MLA_NOPE + MLA_ROPE)
    parts = [w3[..., :MLA_NOPE], w3[..., MLA_NOPE:MLA_NOPE + hr], w3[..., MLA_NOPE + hr:]]
    return jnp.concatenate([p.reshape(w.shape[0], -1) for p in parts], axis=-1)


def _ukv_cols(w):
    w3 = w.reshape(w.shape[0], MLA_HEADS, MLA_NOPE + MLA_V)
    return jnp.concatenate([w3[..., :MLA_NOPE].reshape(w.shape[0], -1), w3[..., MLA_NOPE:].reshape(w.shape[0], -1)], axis=-1)


def _odd_cols(w):
    pad = jnp.zeros(w.shape[:-1] + (LANES - 4 * ML_HEADS,), w.dtype)
    return jnp.concatenate([w[..., :2064], pad, w[..., 2064:]], axis=-1)


def _rope_tables(n_lat, n_ctx):
    t = jnp.arange(n_lat)
    row = (t // GRID_W).astype(F32)
    col = (t % GRID_W).astype(F32)

    def cs(dim):
        nf = dim // 4
        freqs = ROPE_BASE ** (-jnp.arange(nf, dtype=F32) / nf)
        ang = jnp.concatenate([row[:, None] * freqs, col[:, None] * freqs], axis=-1)
        return jnp.cos(ang), jnp.sin(ang)

    ca, sa = cs(DA_QK)
    cm, sm = cs(MLA_ROPE)
    lat = jnp.concatenate([jnp.tile(ca, (1, 4)), jnp.tile(sa, (1, 4)), jnp.tile(cm, (1, 8)), jnp.tile(sm, (1, 8)),
                           jnp.tile(jnp.concatenate([-sm, sm], axis=-1), (1, 4))], axis=-1)
    one, zero = jnp.ones((n_ctx, LANES), F32), jnp.zeros((n_ctx, LANES), F32)
    return lat, jnp.concatenate([one, zero, one, zero, zero], axis=-1)


def _na_bias_tiles(rpb, qb, rows):
    heads, n_dr, n_dc = rpb.shape
    rows_per = qb // GRID_W
    assert rows >= 3 * rows_per and NA_ROWS == 2 * rows_per
    kc, qc = np.arange(GRID_W)[:, None], np.arange(GRID_W)[None, :]
    c0 = np.clip(qc - NA_COLS // 2, 0, GRID_W - NA_COLS)
    col_ok = (kc >= c0) & (kc < c0 + NA_COLS)
    onehot = ((kc - qc + NA_COLS - 1)[None] == np.arange(n_dc)[:, None, None]) & col_ok[None]
    tiles = jnp.einsum('hrj,jp->hrp', rpb * LOG2E, jnp.asarray(onehot.reshape(n_dc, -1), F32),
                       precision=lax.Precision.HIGHEST).reshape(heads, n_dr, GRID_W, GRID_W)
    tiles = jnp.where(jnp.asarray(col_ok)[None, None], tiles, MASK_VALUE)
    kr = (np.arange(3 * rows_per) - rows_per)[:, None]
    qr = np.arange(rows_per)[None, :]
    dr = kr - qr
    window = [(kr >= 0) & (kr < NA_ROWS) & (qr >= 0),
              (dr >= -(NA_ROWS // 2)) & (dr < NA_ROWS // 2),
              (kr >= rows_per - NA_ROWS) & (kr < rows_per) & (qr >= 0)]
    masked = jnp.full((heads, GRID_W, GRID_W), MASK_VALUE, F32)
    classes = []
    for ok in window:
        key_rows = [jnp.concatenate([tiles[:, dr[a, b] + NA_ROWS - 1] if ok[a, b] else masked
                                     for b in range(rows_per)], axis=-1) for a in range(3 * rows_per)]
        classes.append(jnp.concatenate(key_rows, axis=1))
    return jnp.stack(classes)


def _tile_plan(n_lat, n_ctx):
    return {
        "tok_lat": 512,
        "tok_odd": 256,
        "tok_ctx": min(256, n_ctx),
        "attn_q": 1024 if n_lat % 1024 == 0 else 512,
        "attn_k": 1024 if n_lat % 1024 == 0 else 512,
        "na_q": 4 * GRID_W,
    }


def kernel(x, c, ctx, c_ctx, ada_w, ada_b, ln_g, ln_b, ev_w_in, ev_b_in, da_lambda, da_subln_g, mla_q_norm_g, mla_kv_norm_g, mla_w_uq, mla_w_ukv, ev_w_out, od_w_in, od_b_in, ml_conv_w, ml_conv_b, ml_f_bias, ml_norm_g, na_rpb, od_w_out):
    bsz, n_lat, d = x.shape
    n_ctx = ctx.shape[1]
    rows = n_lat // GRID_W
    assert d == D_MODEL and bsz + 1 <= 8 and n_lat % 512 == 0 and n_ctx % 256 == 0 and rows >= NA_ROWS

    cond8 = jnp.zeros((8, d), F32).at[:bsz].set(c).at[bsz].set(c_ctx)
    mod_all = _ada_mod(cond8, ada_w, ada_b)
    rope_lat, rope_ctx = _rope_tables(n_lat, n_ctx)

    tp = _tile_plan(n_lat, n_ctx)
    tm_lat, tm_ctx, tq, tk, na_qb = tp["tok_lat"], tp["tok_ctx"], tp["attn_q"], tp["attn_k"], tp["na_q"]
    xl, xc = x, ctx
    for l in range(DEPTH):
        update_ctx = l < DEPTH - 1
        i = l // 2
        mod_l = mod_all[l, :bsz].reshape(bsz, 1, 3 * d)
        mod_c = mod_all[l, bsz].reshape(1, 1, 3 * d)
        lng, lnb = ln_g[l].reshape(1, d), ln_b[l].reshape(1, d)
        if l % 2 == 0:
            lam_init = 0.8 - 0.6 * math.exp(-0.3 * l)
            w = _even_cols(ev_w_in[i].astype(BF16))
            b = _even_cols(ev_b_in[i]).reshape(1, EV_COLS)
            wuq = _uq_cols(mla_w_uq[i].astype(BF16))
            wukv = _ukv_cols(mla_w_ukv[i].astype(BF16))
            qg = mla_q_norm_g[i].reshape(1, MLA_Q_RANK)
            kvg = mla_kv_norm_g[i].reshape(1, MLA_KV_RANK)
            subg = da_subln_g[i].reshape(DA_V, 1)
            wout = ev_w_out[i].astype(BF16)
            qt_l, ka_l, km_l, vt_l, g_l = _proj_even(xl, mod_l, rope_lat, w, b, qg, kvg, wuq, wukv, tm_lat)
            qt_c, ka_c, km_c, vt_c, g_c = _proj_even(xc, mod_c, rope_ctx, w, b, qg, kvg, wuq, wukv, tm_ctx)
            y_l = _attn_even(da_lambda[i], subg, qt_l, ka_l, km_l, vt_l, (ka_c, km_c, vt_c), lam_init, tq, tk)
            xl_new = _out_proj(_out_even_kernel, "out_even", (y_l, g_l), xl, mod_l, wout, (), lng, lnb, tm_lat)
            if update_ctx:
                y_c = _attn_even(da_lambda[i], subg, qt_c, ka_c, km_c, vt_c, None, lam_init, n_ctx, n_ctx)
                xc = _out_proj(_out_even_kernel, "out_even", (y_c, g_c), xc, mod_c, wout, (), lng, lnb, tm_ctx)
            xl = xl_new
        else:
            w = _odd_cols(od_w_in[i].astype(BF16))
            b = _odd_cols(od_b_in[i]).reshape(1, OD_COLS)
            wout = od_w_out[i].astype(BF16)
            cw, cb = ml_conv_w[i], ml_conv_b[i].reshape(1, 2 * ML_WIDTH)
            fb = ml_f_bias[i].astype(F32)
            fb_lane = jnp.zeros((1, LANES), F32).at[0, 4:8].set(fb[0]).at[0, 12:16].set(fb[1])
            ng = ml_norm_g[i].reshape(1, ML_WIDTH)
            qk_l, v_l, so_l, gt_l, qn_l, kn_l, vn_l, g_l = _proj_odd(xl, mod_l, w, b, tp["tok_odd"])
            qk_c, v_c, so_c, gt_c, qn_c, kn_c, vn_c, g_c = _proj_odd(xc, mod_c, w, b, tm_ctx)
            q_l, kt_l = _conv_silu(qk_l, cw, cb, tp["tok_lat"])
            q_c, kt_c = _conv_silu(qk_c, cw, cb, n_ctx)
            zero = (jnp.zeros((bsz, 2 * ML_HEADS, ML_DH, 2 * ML_DH), F32), jnp.zeros((bsz, 2, 1, LANES), F32))
            hf_c, hb_c, st_c = _mlstm(q_c, kt_c, v_c, gt_c, fb_lane, zero)
            hf_l, hb_l, _ = _mlstm(q_l, kt_l, v_l, gt_l, fb_lane, st_c)
            bias = _na_bias_tiles(na_rpb[i].astype(F32), na_qb, rows)
            na_l = _na_attn(qn_l, kn_l, vn_l, kn_c, vn_c, bias, na_qb)
            xl_new = _out_proj(_out_odd_kernel, "out_odd", (hf_l, hb_l, so_l, na_l, g_l), xl, mod_l, wout, (ng,), lng, lnb, tm_lat)
            if update_ctx:
                na_c = _na_attn(qn_c, None, None, kn_c, vn_c, None, n_ctx)
                xc = _out_proj(_out_odd_kernel, "out_odd", (hf_c, hb_c, so_c, na_c, g_c), xc, mod_c, wout, (ng,), lng, lnb, tm_ctx)
            xl = xl_new
    return xl
```

```python
import functools
import math

import numpy as np
import jax
import jax.numpy as jnp
from jax import lax
from jax.experimental import pallas as pl
from jax.experimental.pallas import tpu as pltpu

F32 = jnp.float32
BF16 = jnp.bfloat16

D_MODEL = 1024
DEPTH = 4
GRID_W = 64
ROPE_BASE = 10000.0
LN_EPS = 1e-5
RMS_EPS = 1e-6
DEEPNORM_ALPHA = (2.0 * DEPTH) ** 0.25

DA_HEADS, DA_QK, DA_V = 4, 64, 128
LOG2E = math.log2(math.e)
DA_SCALE = DA_QK ** -0.5 * LOG2E
MLA_HEADS, MLA_Q_RANK, MLA_KV_RANK, MLA_NOPE, MLA_ROPE, MLA_V = 8, 256, 128, 64, 32, 64
MLA_SCALE = (MLA_NOPE + MLA_ROPE) ** -0.5 * LOG2E
ML_HEADS, ML_DH, ML_CONV, ML_CHUNK = 4, 128, 5, 128
ML_WIDTH = ML_HEADS * ML_DH
NA_HEADS, NA_DH, NA_ROWS, NA_COLS = 8, 64, 8, 16
NA_WIDTH = NA_HEADS * NA_DH
NA_SCALE = NA_DH ** -0.5 * LOG2E

LANES = 128
MASK_VALUE = -1e30
VMEM_LIMIT = 48 * 1024 * 1024

EV_COLS = 3200
EV_V, EV_CQ, EV_CKV, EV_KRA, EV_KRB, EV_GATE = 1024, 1536, 1792, 1920, 2048, 2176
OD_COLS = 4736
OD_V, OD_O, OD_GATES, OD_QN, OD_KN, OD_VN, OD_GATE = 1024, 1536, 2048, 2176, 2688, 3200, 3712
ONES_ROWS = 16
VA_ROWS = DA_V + ONES_ROWS
VM_ROWS = MLA_V + ONES_ROWS
VT_ROWS = DA_HEADS * VA_ROWS + MLA_HEADS * VM_ROWS
QT_ROWS = 1280
N_UNITS = 16
VN_ROWS = LANES + ONES_ROWS


def _cparams(sem):
    return pltpu.CompilerParams(dimension_semantics=sem, vmem_limit_bytes=VMEM_LIMIT)


def _silu(x):
    return x * jax.nn.sigmoid(x)


def _dot(a, b):
    return jnp.dot(a, b, preferred_element_type=F32)


def _log_sigmoid(x):
    return jnp.minimum(x, 0.0) - jnp.log(1.0 + jnp.exp(-jnp.abs(x)))


def _ada_kernel(c_ref, w_ref, b_ref, o_ref):
    c = c_ref[...]
    o_ref[0] = _dot(_silu(c).astype(BF16), w_ref[0].astype(BF16)) + b_ref[0]


def _ada_mod(cond8, ada_w, ada_b):
    depth, d, e = ada_w.shape
    tn = 1024
    return pl.pallas_call(
        _ada_kernel,
        grid=(depth, e // tn),
        in_specs=[pl.BlockSpec((8, d), lambda l, n: (0, 0)),
                  pl.BlockSpec((1, d, tn), lambda l, n: (l, 0, n)),
                  pl.BlockSpec((1, 1, tn), lambda l, n: (l, 0, n))],
        out_specs=pl.BlockSpec((1, 8, tn), lambda l, n: (l, 0, n)),
        out_shape=jax.ShapeDtypeStruct((depth, 8, e), F32),
        compiler_params=_cparams(("parallel", "parallel")),
        name="ada_mod",
    )(cond8, ada_w, ada_b.reshape(depth, 1, e))


def _mod_spec(mod):
    per_batch = mod.shape[0] > 1
    return pl.BlockSpec((1, 1, mod.shape[2]), (lambda b, i: (b, 0, 0)) if per_batch else (lambda b, i: (0, 0, 0)))


def _proj_even_kernel(x_ref, mod_ref, rope_ref, w_ref, b_ref, qg_ref, kvg_ref, wuq_ref, wukv_ref,
                      qt_ref, ka_ref, km_ref, vt_ref, g_ref):
    d = D_MODEL
    x = x_ref[0]
    mod = mod_ref[0]
    h = (x * (1.0 + mod[:, d:2 * d]) + mod[:, 0:d]).astype(BF16)
    z = _dot(h, w_ref[...]) + b_ref[...]
    rope = rope_ref[...]
    cos_a, sin_a, cos_m, sin_m, sin_ms = (rope[:, LANES * i:LANES * (i + 1)] for i in range(5))

    def blk(o):
        return z[:, o:o + LANES]

    def rot(a, b, c, s):
        return a * c - b * s, b * c + a * s

    q1a, q1b = rot(blk(0), blk(128), cos_a, sin_a)
    q2a, q2b = rot(blk(256), blk(384), cos_a, sin_a)
    k1a, k1b = rot(blk(512), blk(640), cos_a, sin_a)
    k2a, k2b = rot(blk(768), blk(896), cos_a, sin_a)
    for i, t in enumerate((q1a, q1b, q2a, q2b)):
        qt_ref[0, LANES * i:LANES * (i + 1), :] = (t * DA_SCALE).T.astype(BF16)
    for i, t in enumerate((k1a, k1b, k2a, k2b)):
        ka_ref[0, i // 2, :, LANES * (i % 2):LANES * (i % 2 + 1)] = t.astype(BF16)
    for i in range(4):
        vt_ref[0, VA_ROWS * i:VA_ROWS * i + DA_V, :] = blk(EV_V + LANES * i).T.astype(BF16)
        vt_ref[0, VA_ROWS * i + DA_V:VA_ROWS * (i + 1), :] = jnp.ones((ONES_ROWS, x.shape[0]), BF16)

    def rms(t, g):
        return (t * lax.rsqrt(jnp.mean(t * t, axis=-1, keepdims=True) + RMS_EPS) * g).astype(BF16)

    qm = _dot(rms(z[:, EV_CQ:EV_CQ + MLA_Q_RANK], qg_ref[...]), wuq_ref[...])
    qra, qrb = rot(qm[:, 512:640], qm[:, 640:768], cos_m, sin_m)
    for i in range(4):
        qt_ref[0, 512 + LANES * i:512 + LANES * (i + 1), :] = (qm[:, LANES * i:LANES * (i + 1)] * MLA_SCALE).T.astype(BF16)
    qt_ref[0, 1024:1152, :] = (qra * MLA_SCALE).T.astype(BF16)
    qt_ref[0, 1152:1280, :] = (qrb * MLA_SCALE).T.astype(BF16)

    kvm = _dot(rms(z[:, EV_CKV:EV_CKV + MLA_KV_RANK], kvg_ref[...]), wukv_ref[...])
    k_rope = (blk(EV_KRA) * cos_m + blk(EV_KRB) * sin_ms).astype(BF16)
    for g in range(4):
        km_ref[0, g, :, 0:LANES] = kvm[:, LANES * g:LANES * (g + 1)].astype(BF16)
        km_ref[0, g, :, LANES:2 * LANES] = k_rope
        vm_t = kvm[:, 512 + LANES * g:512 + LANES * (g + 1)].T.astype(BF16)
        for half in range(2):
            r0 = DA_HEADS * VA_ROWS + VM_ROWS * (2 * g + half)
            vt_ref[0, r0:r0 + MLA_V, :] = vm_t[MLA_V * half:MLA_V * (half + 1), :]
            vt_ref[0, r0 + MLA_V:r0 + VM_ROWS, :] = jnp.ones((ONES_ROWS, x.shape[0]), BF16)

    g_ref[0] = _silu(z[:, EV_GATE:EV_GATE + d]).astype(BF16)


def _proj_even(x, mod, rope, w, b, qg, kvg, wuq, wukv, tm):
    bsz, n, d = x.shape
    const = lambda shape: pl.BlockSpec(shape, lambda b_, i: (0,) * len(shape))
    return pl.pallas_call(
        _proj_even_kernel,
        grid=(bsz, n // tm),
        in_specs=[pl.BlockSpec((1, tm, d), lambda b_, i: (b_, i, 0)),
                  _mod_spec(mod),
                  pl.BlockSpec((tm, 5 * LANES), lambda b_, i: (i, 0)),
                  const((d, EV_COLS)), const((1, EV_COLS)), const((1, MLA_Q_RANK)), const((1, MLA_KV_RANK)),
                  const((MLA_Q_RANK, 768)), const((MLA_KV_RANK, 1024))],
        out_specs=[pl.BlockSpec((1, QT_ROWS, tm), lambda b_, i: (b_, 0, i)),
                   pl.BlockSpec((1, 2, tm, 256), lambda b_, i: (b_, 0, i, 0)),
                   pl.BlockSpec((1, 4, tm, 256), lambda b_, i: (b_, 0, i, 0)),
                   pl.BlockSpec((1, VT_ROWS, tm), lambda b_, i: (b_, 0, i)),
                   pl.BlockSpec((1, tm, d), lambda b_, i: (b_, i, 0))],
        out_shape=[jax.ShapeDtypeStruct((bsz, QT_ROWS, n), BF16),
                   jax.ShapeDtypeStruct((bsz, 2, n, 256), BF16),
                   jax.ShapeDtypeStruct((bsz, 4, n, 256), BF16),
                   jax.ShapeDtypeStruct((bsz, VT_ROWS, n), BF16),
                   jax.ShapeDtypeStruct((bsz, n, d), BF16)],
        compiler_params=_cparams(("parallel", "parallel")),
        name="proj_even",
    )(x, mod, rope, w, b, qg, kvg, wuq, wukv)


def _attn_even_kernel(*refs, tq, nkv, sub_tk, has_ctx, lam_init):
    if has_ctx:
        (lam_ref, subg_ref, qt_ref, ka_ref, km_ref, vt_ref, kac_ref, kmc_ref, vtc_ref,
         y_ref, qw, acc_a, acc_m, m_s) = refs
    else:
        (lam_ref, subg_ref, qt_ref, ka_ref, km_ref, vt_ref,
         y_ref, qw, acc_a, acc_m, m_s) = refs
    j = pl.program_id(2)

    def process(ka, km, vt, n_sub, sub):
        def scores(t, u):
            rows = slice(sub * t, sub * (t + 1))
            k = ka[0, u // DA_HEADS, rows, :] if u < 2 * DA_HEADS else km[0, (u - 2 * DA_HEADS) // 2, rows, :]
            return _dot(k, qw[u])

        def softmax_step(u, s):
            m_old = m_s[u]
            m_new = jnp.maximum(m_old, jnp.max(s, axis=0, keepdims=True))
            alpha = jnp.exp2(m_old - m_new)
            p = jnp.exp2(s - m_new)
            m_s[u] = m_new
            return alpha, p.astype(BF16)

        def pv(t, u, alpha, p):
            cols = slice(sub * t, sub * (t + 1))
            if u < 2 * DA_HEADS:
                hd = u % DA_HEADS
                acc_a[u] = alpha * acc_a[u] + _dot(vt[0, VA_ROWS * hd:VA_ROWS * (hd + 1), cols], p)
            else:
                hd = u - 2 * DA_HEADS
                v = vt[0, DA_HEADS * VA_ROWS + VM_ROWS * hd:DA_HEADS * VA_ROWS + VM_ROWS * (hd + 1), cols]
                acc_m[hd] = alpha * acc_m[hd] + _dot(v, p)

        pairs = [(t, u) for t in range(n_sub) for u in range(N_UNITS)]
        s_next = scores(*pairs[0])
        pending = None
        for i, (t, u) in enumerate(pairs):
            s_cur = s_next
            if i + 1 < len(pairs):
                s_next = scores(*pairs[i + 1])
            if pending is not None:
                pv(*pending)
            alpha, p = softmax_step(u, s_cur)
            pending = (t, u, alpha, p)
        pv(*pending)

    @pl.when(j == 0)
    def _init():
        row = lax.broadcasted_iota(jnp.int32, (256, tq), 0)
        head_of_row = (row >> 5) & (DA_HEADS - 1)
        for mp in range(2):
            q = qt_ref[0, 256 * mp:256 * (mp + 1), :]
            for hd in range(DA_HEADS):
                qw[mp * DA_HEADS + hd] = jnp.where(head_of_row == hd, q, jnp.zeros_like(q))
        for hd in range(MLA_HEADS):
            u = 2 * DA_HEADS + hd
            qw[u] = jnp.zeros((256, tq), BF16)
            qw[u, 64 * (hd % 2):64 * (hd % 2) + 64, :] = qt_ref[0, 512 + 64 * hd:512 + 64 * hd + 64, :]
            qw[u, 128:144, :] = qt_ref[0, 1024 + 16 * hd:1024 + 16 * hd + 16, :]
            qw[u, 144:160, :] = qt_ref[0, 1152 + 16 * hd:1152 + 16 * hd + 16, :]
        acc_a[...] = jnp.zeros(acc_a.shape, F32)
        acc_m[...] = jnp.zeros(acc_m.shape, F32)
        m_s[...] = jnp.full(m_s.shape, MASK_VALUE, F32)
        if has_ctx:
            process(kac_ref, kmc_ref, vtc_ref, 1, kac_ref.shape[2])

    process(ka_ref, km_ref, vt_ref, ka_ref.shape[2] // sub_tk, sub_tk)

    @pl.when(j == nkv - 1)
    def _finish():
        lm = lam_ref[...]
        lam = (jnp.exp(jnp.sum(lm[0:1] * lm[1:2], axis=-1, keepdims=True))
               - jnp.exp(jnp.sum(lm[2:3] * lm[3:4], axis=-1, keepdims=True)) + lam_init)
        for hd in range(DA_HEADS):
            a1, a2 = acc_a[hd], acc_a[DA_HEADS + hd]
            o = a1[0:DA_V] / a1[DA_V:DA_V + 1] - lam * (a2[0:DA_V] / a2[DA_V:DA_V + 1])
            o = o * lax.rsqrt(jnp.mean(o * o, axis=0, keepdims=True) + RMS_EPS) * subg_ref[...] * (1.0 - lam_init)
            y_ref[0, :, DA_V * hd:DA_V * (hd + 1)] = o.T.astype(BF16)
        for g in range(MLA_HEADS // 2):
            b1, b2 = acc_m[2 * g], acc_m[2 * g + 1]
            o = jnp.concatenate([b1[0:MLA_V] / b1[MLA_V:MLA_V + 1], b2[0:MLA_V] / b2[MLA_V:MLA_V + 1]], axis=0)
            y_ref[0, :, 512 + LANES * g:512 + LANES * (g + 1)] = o.T.astype(BF16)


def _attn_even(lam_p, subg, qt, ka, km, vt, ctx_kv, lam_init, tq, tk):
    bsz, _, n = qt.shape
    nk = ka.shape[2]
    nq, nkv = n // tq, nk // tk
    has_ctx = ctx_kv is not None
    in_specs = [pl.BlockSpec((4, DA_QK), lambda b, i, j: (0, 0)),
                pl.BlockSpec((DA_V, 1), lambda b, i, j: (0, 0)),
                pl.BlockSpec((1, QT_ROWS, tq), lambda b, i, j: (b, 0, i)),
                pl.BlockSpec((1, 2, tk, 256), lambda b, i, j: (b, 0, j, 0)),
                pl.BlockSpec((1, 4, tk, 256), lambda b, i, j: (b, 0, j, 0)),
                pl.BlockSpec((1, VT_ROWS, tk), lambda b, i, j: (b, 0, j))]
    args = [lam_p, subg, qt, ka, km, vt]
    if has_ctx:
        tc = ctx_kv[0].shape[2]
        in_specs += [pl.BlockSpec((1, 2, tc, 256), lambda b, i, j: (b, 0, 0, 0)),
                     pl.BlockSpec((1, 4, tc, 256), lambda b, i, j: (b, 0, 0, 0)),
                     pl.BlockSpec((1, VT_ROWS, tc), lambda b, i, j: (b, 0, 0))]
        args += list(ctx_kv)
    return pl.pallas_call(
        functools.partial(_attn_even_kernel, tq=tq, nkv=nkv, sub_tk=min(tk, 512), has_ctx=has_ctx, lam_init=lam_init),
        grid=(bsz, nq, nkv),
        in_specs=in_specs,
        out_specs=pl.BlockSpec((1, tq, D_MODEL), lambda b, i, j: (b, i, 0)),
        out_shape=jax.ShapeDtypeStruct((bsz, n, D_MODEL), BF16),
        scratch_shapes=[pltpu.VMEM((N_UNITS, 256, tq), BF16),
                        pltpu.VMEM((2 * DA_HEADS, VA_ROWS, tq), F32),
                        pltpu.VMEM((MLA_HEADS, VM_ROWS, tq), F32),
                        pltpu.VMEM((N_UNITS, 1, tq), F32)],
        compiler_params=_cparams(("parallel", "parallel", "arbitrary")),
        name="attn_even",
    )(*args)


def _layer_norm_rows(t, g, b):
    mu = jnp.mean(t, axis=-1, keepdims=True)
    tc = t - mu
    var = jnp.mean(tc * tc, axis=-1, keepdims=True)
    return tc * lax.rsqrt(var + LN_EPS) * g + b


def _out_even_kernel(y_ref, g_ref, x_ref, mod_ref, w_ref, lng_ref, lnb_ref, o_ref):
    d = D_MODEL
    y = (y_ref[0].astype(F32) * g_ref[0].astype(F32)).astype(BF16)
    out = _dot(y, w_ref[...])
    t = DEEPNORM_ALPHA * x_ref[0] + mod_ref[0][:, 2 * d:3 * d] * out
    o_ref[0] = _layer_norm_rows(t, lng_ref[...], lnb_ref[...])


def _out_odd_kernel(hf_ref, hb_ref, so_ref, na_ref, g_ref, x_ref, mod_ref, w_ref, ng_ref, lng_ref, lnb_ref, o_ref):
    d = D_MODEL
    hsum = hf_ref[0].astype(F32) + hb_ref[0].astype(F32)
    parts = []
    for hd in range(ML_HEADS):
        t = hsum[:, ML_DH * hd:ML_DH * (hd + 1)]
        mu = jnp.mean(t, axis=-1, keepdims=True)
        tc = t - mu
        var = jnp.mean(tc * tc, axis=-1, keepdims=True)
        parts.append(tc * lax.rsqrt(var + LN_EPS))
    y_ml = jnp.concatenate(parts, axis=-1) * ng_ref[...] * so_ref[0].astype(F32)
    g = g_ref[0].astype(F32)
    y = jnp.concatenate([y_ml * g[:, 0:ML_WIDTH], na_ref[0].astype(F32) * g[:, ML_WIDTH:]], axis=-1).astype(BF16)
    out = _dot(y, w_ref[...])
    t = DEEPNORM_ALPHA * x_ref[0] + mod_ref[0][:, 2 * d:3 * d] * out
    o_ref[0] = _layer_norm_rows(t, lng_ref[...], lnb_ref[...])


def _out_proj(kernel_fn, name, acts, x, mod, w, extra, ln_g, ln_b, tm):
    bsz, n, d = x.shape
    tok = lambda a: pl.BlockSpec((1, tm, a.shape[2]), lambda b_, i: (b_, i, 0))
    const = lambda a: pl.BlockSpec(a.shape, lambda b_, i: (0,) * a.ndim)
    ins = list(acts) + [x, mod, w] + list(extra) + [ln_g, ln_b]
    specs = [tok(a) for a in acts] + [tok(x), _mod_spec(mod), const(w)] + [const(e) for e in extra] + [const(ln_g), const(ln_b)]
    return pl.pallas_call(
        kernel_fn,
        grid=(bsz, n // tm),
        in_specs=specs,
        out_specs=pl.BlockSpec((1, tm, d), lambda b_, i: (b_, i, 0)),
        out_shape=jax.ShapeDtypeStruct((bsz, n, d), F32),
        compiler_params=_cparams(("parallel", "parallel")),
        name=name,
    )(*ins)


def _proj_odd_kernel(x_ref, xp_ref, xn_ref, mod_ref, w_ref, b_ref, cw_ref, cb_ref,
                     q_ref, kt_ref, v_ref, so_ref, gt_ref, qn_ref, kn_ref, vn_ref, g_ref, buf, *, tm, nt):
    d = D_MODEL
    i = pl.program_id(1)
    mod = mod_ref[0]
    x = jnp.concatenate([xp_ref[0], x_ref[0], xn_ref[0]], axis=0)
    h = (x * (1.0 + mod[:, d:2 * d]) + mod[:, 0:d]).astype(BF16)
    z_all = _dot(h, w_ref[...]) + b_ref[...]
    z = z_all[8:8 + tm]
    v_ref[0] = z[:, OD_V:OD_V + ML_WIDTH].astype(BF16)
    so_ref[0] = jax.nn.sigmoid(z[:, OD_O:OD_O + ML_WIDTH]).astype(BF16)
    gt_ref[0] = z[:, OD_GATES:OD_GATES + LANES]
    kn_ref[0] = z[:, OD_KN:OD_KN + NA_WIDTH].astype(BF16)
    for g in range(NA_HEADS // 2):
        qn_ref[0, LANES * g:LANES * (g + 1), :] = (z[:, OD_QN + LANES * g:OD_QN + LANES * (g + 1)] * NA_SCALE).T.astype(BF16)
        vn_ref[0, VN_ROWS * g:VN_ROWS * g + LANES, :] = z[:, OD_VN + LANES * g:OD_VN + LANES * (g + 1)].T.astype(BF16)
        vn_ref[0, VN_ROWS * g + LANES:VN_ROWS * (g + 1), :] = jnp.ones((ONES_ROWS, tm), BF16)
    g_ref[0] = _silu(z[:, OD_GATE:OD_GATE + d]).astype(BF16)

    c = 2 * ML_WIDTH
    buf[0:8, :] = jnp.where(i > 0, z_all[0:8, 0:c], 0.0)
    buf[8:8 + tm, :] = z[:, 0:c]
    buf[8 + tm:16 + tm, :] = jnp.where(i < nt - 1, z_all[8 + tm:16 + tm, 0:c], 0.0)
    cw = cw_ref[...]
    acc = cb_ref[...] + buf[6:6 + tm, :] * cw[0:1]
    for tap in range(1, ML_CONV):
        acc = acc + buf[6 + tap:6 + tap + tm, :] * cw[tap:tap + 1]
    y = _silu(acc)
    q_ref[0] = y[:, 0:ML_WIDTH].astype(BF16)
    for hd in range(ML_HEADS):
        k = y[:, ML_WIDTH + ML_DH * hd:ML_WIDTH + ML_DH * (hd + 1)] * (ML_DH ** -0.5)
        kt_ref[0, ML_DH * hd:ML_DH * (hd + 1), :] = k.T.astype(BF16)


def _proj_odd(x, mod, w, b, cw, cb, tm):
    bsz, n, d = x.shape
    nt = n // tm
    r = tm // 8
    last8 = n // 8 - 1
    const = lambda shape: pl.BlockSpec(shape, lambda b_, i: (0,) * len(shape))
    tok = lambda wd, dt: (pl.BlockSpec((1, tm, wd), lambda b_, i: (b_, i, 0)), jax.ShapeDtypeStruct((bsz, n, wd), dt))
    feat = lambda rws, dt: (pl.BlockSpec((1, rws, tm), lambda b_, i: (b_, 0, i)), jax.ShapeDtypeStruct((bsz, rws, n), dt))
    outs = [tok(ML_WIDTH, BF16), feat(ML_WIDTH, BF16), tok(ML_WIDTH, BF16), tok(ML_WIDTH, BF16), tok(LANES, F32),
            feat(NA_WIDTH, BF16), tok(NA_WIDTH, BF16), feat(VN_ROWS * NA_HEADS // 2, BF16), tok(d, BF16)]
    return pl.pallas_call(
        functools.partial(_proj_odd_kernel, tm=tm, nt=nt),
        grid=(bsz, nt),
        in_specs=[pl.BlockSpec((1, tm, d), lambda b_, i: (b_, i, 0)),
                  pl.BlockSpec((1, 8, d), lambda b_, i: (b_, jnp.maximum(i * r - 1, 0), 0)),
                  pl.BlockSpec((1, 8, d), lambda b_, i: (b_, jnp.minimum((i + 1) * r, last8), 0)),
                  _mod_spec(mod), const((d, OD_COLS)), const((1, OD_COLS)),
                  const((ML_CONV, 2 * ML_WIDTH)), const((1, 2 * ML_WIDTH))],
        out_specs=[o[0] for o in outs],
        out_shape=[o[1] for o in outs],
        scratch_shapes=[pltpu.VMEM((tm + 16, 2 * ML_WIDTH), F32)],
        compiler_params=_cparams(("parallel", "parallel")),
        name="proj_odd",
    )(x, x, x, mod, w, b, cw, cb)


def _scan_rows(x, reverse, op, fill):
    n = x.shape[0]
    row = lax.broadcasted_iota(jnp.int32, x.shape, 0)
    sh = 1
    while sh < n:
        if reverse:
            x = op(x, jnp.where(row + sh < n, pltpu.roll(x, n - sh, 0), fill))
        else:
            x = op(x, jnp.where(row >= sh, pltpu.roll(x, sh, 0), fill))
        sh *= 2
    return x


def _mlstm_kernel(qf_ref, ktf_ref, vf_ref, gf_ref, qb_ref, ktb_ref, vb_ref, gb_ref, fb_ref, c0_ref, m0_ref,
                  hf_ref, hb_ref, c1_ref, m1_ref, c_s, m_s, *, nc):
    L, d = ML_CHUNK, ML_DH
    step = pl.program_id(1)

    @pl.when(step == 0)
    def _load_state():
        c_s[...] = c0_ref[0]
        m_s[...] = m0_ref[0]

    lane = lax.broadcasted_iota(jnp.int32, (L, LANES), 1)
    is_forget = (lane & ML_HEADS) != 0
    ti = lax.broadcasted_iota(jnp.int32, (L, L), 0)
    si = lax.broadcasted_iota(jnp.int32, (L, L), 1)
    ones = jnp.ones((L, d), BF16)
    dirs = ((False, qf_ref, ktf_ref, vf_ref, gf_ref, hf_ref), (True, qb_ref, ktb_ref, vb_ref, gb_ref, hb_ref))
    qk_all = [_dot(q_ref[0, :, d * hd:d * (hd + 1)], kt_ref[0, d * hd:d * (hd + 1), :])
              for (_, q_ref, kt_ref, _, _, _) in dirs for hd in range(ML_HEADS)]
    gate = []
    for di, (reverse, q_ref, kt_ref, v_ref, g_ref, h_ref) in enumerate(dirs):
        gates = g_ref[0]
        g2 = jnp.where(is_forget, _log_sigmoid(gates + fb_ref[...]), gates)
        a_cum = _scan_rows(jnp.where(is_forget, g2, 0.0), reverse, jnp.add, 0.0)
        b_t = pltpu.roll(g2, ML_HEADS, 1) - a_cum
        last = 0 if reverse else L - 1
        a_tot = a_cum[last:last + 1, :]
        m_old = m_s[di]
        mm = jnp.maximum(m_old, jnp.max(b_t, axis=0, keepdims=True))
        w_s_t = jnp.exp(b_t - mm).T
        decay = jnp.exp(m_old - mm)
        m_s[di] = a_tot + mm
        big_m = jnp.maximum(m_old, _scan_rows(b_t, reverse, jnp.maximum, MASK_VALUE))
        m_t = a_cum + big_m
        causal = (ti <= si) if reverse else (ti >= si)
        gate.append((m_old, w_s_t, decay, big_m, m_t, b_t.T, causal))

    results = []
    for di, (reverse, q_ref, kt_ref, v_ref, g_ref, h_ref) in enumerate(dirs):
        m_old, w_s_t, decay, big_m, m_t, b_rows, causal = gate[di]
        for hd in range(ML_HEADS):
            f = 2 * ML_HEADS * di + ML_HEADS + hd
            ch = ML_HEADS * di + hd
            q = q_ref[0, :, d * hd:d * (hd + 1)]
            kt = kt_ref[0, d * hd:d * (hd + 1), :]
            vext = jnp.concatenate([v_ref[0, :, d * hd:d * (hd + 1)], ones], axis=-1)
            c_old = c_s[ch]
            big_m_b = jnp.broadcast_to(big_m[:, f:f + 1], (L, L))
            m_old_b = jnp.broadcast_to(m_old[:, f:f + 1], (L, d))
            w_ts = jnp.exp(jnp.where(causal, b_rows[f:f + 1, :] - big_m_b, MASK_VALUE)) * qk_all[ch]
            w_c = jnp.exp(m_old_b - big_m_b)
            lhs = jnp.concatenate([w_ts.astype(BF16), (q.astype(F32) * w_c).astype(BF16)], axis=-1)
            rhs = jnp.concatenate([vext, c_old.astype(BF16)], axis=0)
            out = _dot(lhs, rhs)
            kw_t = (kt.astype(F32) * w_s_t[f:f + 1, :]).astype(BF16)
            m_t_b = jnp.broadcast_to(m_t[:, f:f + 1], (L, d))
            results.append((out, m_t_b, decay[:, f:f + 1] * c_old + _dot(kw_t, vext)))

    for di, (reverse, q_ref, kt_ref, v_ref, g_ref, h_ref) in enumerate(dirs):
        for hd in range(ML_HEADS):
            ch = ML_HEADS * di + hd
            out, m_t_b, c_new = results[ch]
            h = out[:, 0:d] / jnp.maximum(jnp.abs(out[:, d:]), jnp.exp(-m_t_b))
            h_ref[0, :, d * hd:d * (hd + 1)] = h.astype(BF16)
            c_s[ch] = c_new

    @pl.when(step == nc - 1)
    def _store_state():
        c1_ref[0] = c_s[...]
        m1_ref[0] = m_s[...]


def _mlstm(q, kt, v, gates, fb_lane, state):
    bsz, n, _ = q.shape
    nc = n // ML_CHUNK
    L, w = ML_CHUNK, ML_WIDTH
    fwd = lambda b_, s: (b_, s, 0)
    bwd = lambda b_, s: (b_, nc - 1 - s, 0)
    fwd_t = lambda b_, s: (b_, 0, s)
    bwd_t = lambda b_, s: (b_, 0, nc - 1 - s)
    st = lambda b_, s: (b_, 0, 0, 0)
    c0, m0 = state
    c_shape, m_shape = (2 * ML_HEADS, ML_DH, 2 * ML_DH), (2, 1, LANES)
    tok = lambda width, im: pl.BlockSpec((1, L, width), im)
    out = pl.pallas_call(
        functools.partial(_mlstm_kernel, nc=nc),
        grid=(bsz, nc),
        in_specs=[tok(w, fwd), pl.BlockSpec((1, w, L), fwd_t), tok(w, fwd), tok(LANES, fwd),
                  tok(w, bwd), pl.BlockSpec((1, w, L), bwd_t), tok(w, bwd), tok(LANES, bwd),
                  pl.BlockSpec((1, LANES), lambda b_, s: (0, 0)),
                  pl.BlockSpec((1,) + c_shape, st), pl.BlockSpec((1,) + m_shape, st)],
        out_specs=[tok(w, fwd), tok(w, bwd), pl.BlockSpec((1,) + c_shape, st), pl.BlockSpec((1,) + m_shape, st)],
        out_shape=[jax.ShapeDtypeStruct((bsz, n, w), BF16), jax.ShapeDtypeStruct((bsz, n, w), BF16),
                   jax.ShapeDtypeStruct((bsz,) + c_shape, F32), jax.ShapeDtypeStruct((bsz,) + m_shape, F32)],
        scratch_shapes=[pltpu.VMEM(c_shape, F32), pltpu.VMEM(m_shape, F32)],
        compiler_params=_cparams(("parallel", "arbitrary")),
        name="mlstm",
    )(q, kt, v, gates, q, kt, v, gates, fb_lane, c0, m0)
    return out[0], out[1], (out[2], out[3])


def _na_kernel(*refs, n_blocks):
    if n_blocks:
        q_ref, kc_ref, vc_ref, kp_ref, k0_ref, k1_ref, vp_ref, v0_ref, v1_ref, bias_ref, o_ref = refs
        k_lat, v_lat = (kp_ref, k0_ref, k1_ref), (vp_ref, v0_ref, v1_ref)
    else:
        q_ref, kc_ref, vc_ref, o_ref = refs
        k_lat, v_lat = (), ()
    qb = q_ref.shape[2]
    row = lax.broadcasted_iota(jnp.int32, (LANES, qb), 0)

    def scores(hd):
        g, half = hd // 2, hd % 2
        q2 = q_ref[0, LANES * g:LANES * (g + 1), :]
        q = jnp.where((row >> 6) == half, q2, jnp.zeros_like(q2))
        sl = slice(LANES * g, LANES * (g + 1))
        parts = [_dot(kr[0, :, sl], q) for kr in k_lat]
        if n_blocks:
            bias = bias_ref[0, hd]
            parts = [p + bias[qb * t:qb * (t + 1), :] for t, p in enumerate(parts)]
        parts.append(_dot(kc_ref[0, :, sl], q))
        return parts

    def softmax(parts):
        mx = functools.reduce(jnp.maximum, [jnp.max(p, axis=0, keepdims=True) for p in parts])
        return [jnp.exp2(p - mx).astype(BF16) for p in parts]

    def weighted_values(hd, probs):
        g, half = hd // 2, hd % 2
        rows = slice(VN_ROWS * g, VN_ROWS * (g + 1))
        acc = None
        for p, vr in zip(probs, v_lat + (vc_ref,)):
            t = _dot(vr[0, rows, :], p)
            acc = t if acc is None else acc + t
        return acc[NA_DH * half:NA_DH * (half + 1), :] / acc[LANES:LANES + 1, :]

    all_scores = [scores(hd) for hd in range(NA_HEADS)]
    all_probs = [softmax(parts) for parts in all_scores]
    outs = [weighted_values(hd, probs) for hd, probs in enumerate(all_probs)]
    for g in range(NA_HEADS // 2):
        o_ref[0, :, LANES * g:LANES * (g + 1)] = jnp.concatenate(outs[2 * g:2 * g + 2], axis=0).T.astype(BF16)


def _na_attn(qnt, kn, vnt, kc, vct, bias, qb):
    bsz, w, n = qnt.shape
    t = kc.shape[1]
    n_blocks = 0 if bias is None else 3
    nb = n // qb
    vr = vct.shape[1]
    tok = lambda off: pl.BlockSpec((1, qb, w), lambda b_, i: (b_, jnp.clip(i + off, 0, nb - 1), 0))
    feat = lambda r, off: pl.BlockSpec((1, r, qb), lambda b_, i: (b_, 0, jnp.clip(i + off, 0, nb - 1)))
    ins = [qnt, kc, vct]
    specs = [feat(w, 0), pl.BlockSpec((1, t, w), lambda b_, i: (b_, 0, 0)), pl.BlockSpec((1, vr, t), lambda b_, i: (b_, 0, 0))]
    if n_blocks:
        ins += [kn, kn, kn, vnt, vnt, vnt, bias]
        block_class = lambda b_, i: (jnp.where(i == 0, 0, jnp.where(i == nb - 1, 2, 1)), 0, 0, 0)
        specs += [tok(-1), tok(0), tok(1), feat(vr, -1), feat(vr, 0), feat(vr, 1),
                  pl.BlockSpec((1,) + bias.shape[1:], block_class)]
    return pl.pallas_call(
        functools.partial(_na_kernel, n_blocks=n_blocks),
        grid=(bsz, nb),
        in_specs=specs,
        out_specs=pl.BlockSpec((1, qb, w), lambda b_, i: (b_, i, 0)),
        out_shape=jax.ShapeDtypeStruct((bsz, n, w), BF16),
        compiler_params=_cparams(("parallel", "parallel")),
        name="na_attn" if n_blocks else "ctx_attn",
    )(*ins)


def _even_cols(w):
    lead = w.shape[:-1]
    half, hr = DA_QK // 2, MLA_ROPE // 2
    qk = w[..., :1024].reshape(lead + (4, DA_HEADS, 2, half))
    qk = jnp.swapaxes(qk, -3, -2).reshape(lead + (1024,))
    kr = w[..., 1920:1952]
    pad = jnp.zeros(lead + (LANES - MLA_ROPE,), w.dtype)
    kr_swapped = jnp.concatenate([kr[..., hr:], kr[..., :hr]], axis=-1)
    return jnp.concatenate([qk, w[..., 1024:1920], kr, pad, kr_swapped, pad, w[..., 1952:]], axis=-1)


def _uq_cols(w):
    hr = MLA_ROPE // 2
    w3 = w.reshape(w.shape[0], MLA_HEADS, MLA_NOPE + MLA_ROPE)
    parts = [w3[..., :MLA_NOPE], w3[..., MLA_NOPE:MLA_NOPE + hr], w3[..., MLA_NOPE + hr:]]
    return jnp.concatenate([p.reshape(w.shape[0], -1) for p in parts], axis=-1)


def _ukv_cols(w):
    w3 = w.reshape(w.shape[0], MLA_HEADS, MLA_NOPE + MLA_V)
    return jnp.concatenate([w3[..., :MLA_NOPE].reshape(w.shape[0], -1), w3[..., MLA_NOPE:].reshape(w.shape[0], -1)], axis=-1)


def _odd_cols(w):
    pad = jnp.zeros(w.shape[:-1] + (LANES - 4 * ML_HEADS,), w.dtype)
    return jnp.concatenate([w[..., :2064], pad, w[..., 2064:]], axis=-1)


def _rope_tables(n_lat, n_ctx):
    t = jnp.arange(n_lat)
    row = (t // GRID_W).astype(F32)
    col = (t % GRID_W).astype(F32)

    def cs(dim):
        nf = dim // 4
        freqs = ROPE_BASE ** (-jnp.arange(nf, dtype=F32) / nf)
        ang = jnp.concatenate([row[:, None] * freqs, col[:, None] * freqs], axis=-1)
        return jnp.cos(ang), jnp.sin(ang)

    ca, sa = cs(DA_QK)
    cm, sm = cs(MLA_ROPE)
    lat = jnp.concatenate([jnp.tile(ca, (1, 4)), jnp.tile(sa, (1, 4)), jnp.tile(cm, (1, 8)), jnp.tile(sm, (1, 8)),
                           jnp.tile(jnp.concatenate([-sm, sm], axis=-1), (1, 4))], axis=-1)
    one, zero = jnp.ones((n_ctx, LANES), F32), jnp.zeros((n_ctx, LANES), F32)
    return lat, jnp.concatenate([one, zero, one, zero, zero], axis=-1)


def _na_bias_tiles(rpb, qb, rows):
    heads, n_dr, n_dc = rpb.shape
    rows_per = qb // GRID_W
    assert rows >= 3 * rows_per and NA_ROWS == 2 * rows_per
    kc, qc = np.arange(GRID_W)[:, None], np.arange(GRID_W)[None, :]
    c0 = np.clip(qc - NA_COLS // 2, 0, GRID_W - NA_COLS)
    col_ok = (kc >= c0) & (kc < c0 + NA_COLS)
    onehot = ((kc - qc + NA_COLS - 1)[None] == np.arange(n_dc)[:, None, None]) & col_ok[None]
    tiles = jnp.einsum('hrj,jp->hrp', rpb * LOG2E, jnp.asarray(onehot.reshape(n_dc, -1), F32),
                       precision=lax.Precision.HIGHEST).reshape(heads, n_dr, GRID_W, GRID_W)
    tiles = jnp.where(jnp.asarray(col_ok)[None, None], tiles, MASK_VALUE)
    kr = (np.arange(3 * rows_per) - rows_per)[:, None]
    qr = np.arange(rows_per)[None, :]
    dr = kr - qr
    window = [(kr >= 0) & (kr < NA_ROWS) & (qr >= 0),
              (dr >= -(NA_ROWS // 2)) & (dr < NA_ROWS // 2),
              (kr >= rows_per - NA_ROWS) & (kr < rows_per) & (qr >= 0)]
    masked = jnp.full((heads, GRID_W, GRID_W), MASK_VALUE, F32)
    classes = []
    for ok in window:
        key_rows = [jnp.concatenate([tiles[:, dr[a, b] + NA_ROWS - 1] if ok[a, b] else masked
                                     for b in range(rows_per)], axis=-1) for a in range(3 * rows_per)]
        classes.append(jnp.concatenate(key_rows, axis=1))
    return jnp.stack(classes)


def _tile_plan(n_lat, n_ctx):
    return {
        "tok_lat": 512,
        "tok_odd": 256,
        "tok_ctx": min(256, n_ctx),
        "attn_q": 1024 if n_lat % 1024 == 0 else 512,
        "attn_k": 1024 if n_lat % 1024 == 0 else 512,
        "na_q": 4 * GRID_W,
    }


def kernel(x, c, ctx, c_ctx, ada_w, ada_b, ln_g, ln_b, ev_w_in, ev_b_in, da_lambda, da_subln_g, mla_q_norm_g, mla_kv_norm_g, mla_w_uq, mla_w_ukv, ev_w_out, od_w_in, od_b_in, ml_conv_w, ml_conv_b, ml_f_bias, ml_norm_g, na_rpb, od_w_out):
    bsz, n_lat, d = x.shape
    n_ctx = ctx.shape[1]
    rows = n_lat // GRID_W
    assert d == D_MODEL and bsz + 1 <= 8 and n_lat % 512 == 0 and n_ctx % 256 == 0 and rows >= NA_ROWS

    cond8 = jnp.zeros((8, d), F32).at[:bsz].set(c).at[bsz].set(c_ctx)
    mod_all = _ada_mod(cond8, ada_w, ada_b)
    rope_lat, rope_ctx = _rope_tables(n_lat, n_ctx)

    tp = _tile_plan(n_lat, n_ctx)
    tm_lat, tm_ctx, tq, tk, na_qb = tp["tok_lat"], tp["tok_ctx"], tp["attn_q"], tp["attn_k"], tp["na_q"]
    xl, xc = x, ctx
    for l in range(DEPTH):
        update_ctx = l < DEPTH - 1
        i = l // 2
        mod_l = mod_all[l, :bsz].reshape(bsz, 1, 3 * d)
        mod_c = mod_all[l, bsz].reshape(1, 1, 3 * d)
        lng, lnb = ln_g[l].reshape(1, d), ln_b[l].reshape(1, d)
        if l % 2 == 0:
            lam_init = 0.8 - 0.6 * math.exp(-0.3 * l)
            w = _even_cols(ev_w_in[i].astype(BF16))
            b = _even_cols(ev_b_in[i]).reshape(1, EV_COLS)
            wuq = _uq_cols(mla_w_uq[i].astype(BF16))
            wukv = _ukv_cols(mla_w_ukv[i].astype(BF16))
            qg = mla_q_norm_g[i].reshape(1, MLA_Q_RANK)
            kvg = mla_kv_norm_g[i].reshape(1, MLA_KV_RANK)
            subg = da_subln_g[i].reshape(DA_V, 1)
            wout = ev_w_out[i].astype(BF16)
            qt_l, ka_l, km_l, vt_l, g_l = _proj_even(xl, mod_l, rope_lat, w, b, qg, kvg, wuq, wukv, tm_lat)
            qt_c, ka_c, km_c, vt_c, g_c = _proj_even(xc, mod_c, rope_ctx, w, b, qg, kvg, wuq, wukv, tm_ctx)
            y_l = _attn_even(da_lambda[i], subg, qt_l, ka_l, km_l, vt_l, (ka_c, km_c, vt_c), lam_init, tq, tk)
            xl_new = _out_proj(_out_even_kernel, "out_even", (y_l, g_l), xl, mod_l, wout, (), lng, lnb, tm_lat)
            if update_ctx:
                y_c = _attn_even(da_lambda[i], subg, qt_c, ka_c, km_c, vt_c, None, lam_init, n_ctx, n_ctx)
                xc = _out_proj(_out_even_kernel, "out_even", (y_c, g_c), xc, mod_c, wout, (), lng, lnb, tm_ctx)
            xl = xl_new
        else:
            w = _odd_cols(od_w_in[i].astype(BF16))
            b = _odd_cols(od_b_in[i]).reshape(1, OD_COLS)
            wout = od_w_out[i].astype(BF16)
            cw, cb = ml_conv_w[i], ml_conv_b[i].reshape(1, 2 * ML_WIDTH)
            fb = ml_f_bias[i].astype(F32)
            fb_lane = jnp.zeros((1, LANES), F32).at[0, 4:8].set(fb[0]).at[0, 12:16].set(fb[1])
            ng = ml_norm_g[i].reshape(1, ML_WIDTH)
            q_l, kt_l, v_l, so_l, gt_l, qn_l, kn_l, vn_l, g_l = _proj_odd(xl, mod_l, w, b, cw, cb, tp["tok_odd"])
            q_c, kt_c, v_c, so_c, gt_c, qn_c, kn_c, vn_c, g_c = _proj_odd(xc, mod_c, w, b, cw, cb, tm_ctx)
            zero = (jnp.zeros((bsz, 2 * ML_HEADS, ML_DH, 2 * ML_DH), F32), jnp.zeros((bsz, 2, 1, LANES), F32))
            hf_c, hb_c, st_c = _mlstm(q_c, kt_c, v_c, gt_c, fb_lane, zero)
            hf_l, hb_l, _ = _mlstm(q_l, kt_l, v_l, gt_l, fb_lane, st_c)
            bias = _na_bias_tiles(na_rpb[i].astype(F32), na_qb, rows)
            na_l = _na_attn(qn_l, kn_l, vn_l, kn_c, vn_c, bias, na_qb)
            xl_new = _out_proj(_out_odd_kernel, "out_odd", (hf_l, hb_l, so_l, na_l, g_l), xl, mod_l, wout, (ng,), lng, lnb, tm_lat)
            if update_ctx:
                na_c = _na_attn(qn_c, None, None, kn_c, vn_c, None, n_ctx)
                xc = _out_proj(_out_odd_kernel, "out_odd", (hf_c, hb_c, so_c, na_c, g_c), xc, mod_c, wout, (ng,), lng, lnb, tm_ctx)
            xl = xl_new
    return xl
```

```python
import functools
import math

import numpy as np
import jax
import jax.numpy as jnp
from jax import lax
from jax.experimental import pallas as pl
from jax.experimental.pallas import tpu as pltpu

F32 = jnp.float32
BF16 = jnp.bfloat16

D_MODEL = 1024
DEPTH = 4
GRID_W = 64
ROPE_BASE = 10000.0
LN_EPS = 1e-5
RMS_EPS = 1e-6
DEEPNORM_ALPHA = (2.0 * DEPTH) ** 0.25

DA_HEADS, DA_QK, DA_V = 4, 64, 128
LOG2E = math.log2(math.e)
DA_SCALE = DA_QK ** -0.5 * LOG2E
MLA_HEADS, MLA_Q_RANK, MLA_KV_RANK, MLA_NOPE, MLA_ROPE, MLA_V = 8, 256, 128, 64, 32, 64
MLA_SCALE = (MLA_NOPE + MLA_ROPE) ** -0.5 * LOG2E
ML_HEADS, ML_DH, ML_CONV, ML_CHUNK = 4, 128, 5, 128
ML_WIDTH = ML_HEADS * ML_DH
NA_HEADS, NA_DH, NA_ROWS, NA_COLS = 8, 64, 8, 16
NA_WIDTH = NA_HEADS * NA_DH
NA_SCALE = NA_DH ** -0.5 * LOG2E

LANES = 128
MASK_VALUE = -1e30
VMEM_LIMIT = 56 * 1024 * 1024

EV_COLS = 3200
EV_V, EV_CQ, EV_CKV, EV_KRA, EV_KRB, EV_GATE = 1024, 1536, 1792, 1920, 2048, 2176
OD_COLS = 4736
OD_V, OD_O, OD_GATES, OD_QN, OD_KN, OD_VN, OD_GATE = 1024, 1536, 2048, 2176, 2688, 3200, 3712
ONES_ROWS = 16
VA_ROWS = DA_V + ONES_ROWS
VM_ROWS = MLA_V + ONES_ROWS
VT_ROWS = DA_HEADS * VA_ROWS + MLA_HEADS * VM_ROWS
QT_ROWS = 1280
N_UNITS = 16
VN_ROWS = LANES + ONES_ROWS


def _cparams(sem):
    return pltpu.CompilerParams(dimension_semantics=sem, vmem_limit_bytes=VMEM_LIMIT)


def _silu(x):
    return x * jax.nn.sigmoid(x)


def _dot(a, b):
    return jnp.dot(a, b, preferred_element_type=F32)


def _log_sigmoid(x):
    return jnp.minimum(x, 0.0) - jnp.log(1.0 + jnp.exp(-jnp.abs(x)))


def _ada_kernel(c_ref, w_ref, b_ref, o_ref):
    c = c_ref[...]
    o_ref[0] = _dot(_silu(c).astype(BF16), w_ref[0].astype(BF16)) + b_ref[0]


def _ada_mod(cond8, ada_w, ada_b):
    depth, d, e = ada_w.shape
    tn = 1024
    return pl.pallas_call(
        _ada_kernel,
        grid=(depth, e // tn),
        in_specs=[pl.BlockSpec((8, d), lambda l, n: (0, 0)),
                  pl.BlockSpec((1, d, tn), lambda l, n: (l, 0, n)),
                  pl.BlockSpec((1, 1, tn), lambda l, n: (l, 0, n))],
        out_specs=pl.BlockSpec((1, 8, tn), lambda l, n: (l, 0, n)),
        out_shape=jax.ShapeDtypeStruct((depth, 8, e), F32),
        compiler_params=_cparams(("parallel", "parallel")),
        name="ada_mod",
    )(cond8, ada_w, ada_b.reshape(depth, 1, e))


def _mod_spec(mod):
    per_batch = mod.shape[0] > 1
    return pl.BlockSpec((1, 1, mod.shape[2]), (lambda b, i: (b, 0, 0)) if per_batch else (lambda b, i: (0, 0, 0)))


def _proj_even_kernel(x_ref, mod_ref, rope_ref, w_ref, b_ref, qg_ref, kvg_ref, wuq_ref, wukv_ref,
                      qt_ref, ka_ref, km_ref, vt_ref, g_ref):
    d = D_MODEL
    x = x_ref[0]
    mod = mod_ref[0]
    h = (x * (1.0 + mod[:, d:2 * d]) + mod[:, 0:d]).astype(BF16)
    z = _dot(h, w_ref[...]) + b_ref[...]
    rope = rope_ref[...]
    cos_a, sin_a, cos_m, sin_m, sin_ms = (rope[:, LANES * i:LANES * (i + 1)] for i in range(5))

    def blk(o):
        return z[:, o:o + LANES]

    def rot(a, b, c, s):
        return a * c - b * s, b * c + a * s

    q1a, q1b = rot(blk(0), blk(128), cos_a, sin_a)
    q2a, q2b = rot(blk(256), blk(384), cos_a, sin_a)
    k1a, k1b = rot(blk(512), blk(640), cos_a, sin_a)
    k2a, k2b = rot(blk(768), blk(896), cos_a, sin_a)
    for i, t in enumerate((q1a, q1b, q2a, q2b)):
        qt_ref[0, LANES * i:LANES * (i + 1), :] = (t * DA_SCALE).T.astype(BF16)
    for i, t in enumerate((k1a, k1b, k2a, k2b)):
        ka_ref[0, i // 2, :, LANES * (i % 2):LANES * (i % 2 + 1)] = t.astype(BF16)
    for i in range(4):
        vt_ref[0, VA_ROWS * i:VA_ROWS * i + DA_V, :] = blk(EV_V + LANES * i).T.astype(BF16)
        vt_ref[0, VA_ROWS * i + DA_V:VA_ROWS * (i + 1), :] = jnp.ones((ONES_ROWS, x.shape[0]), BF16)

    def rms(t, g):
        return (t * lax.rsqrt(jnp.mean(t * t, axis=-1, keepdims=True) + RMS_EPS) * g).astype(BF16)

    qm = _dot(rms(z[:, EV_CQ:EV_CQ + MLA_Q_RANK], qg_ref[...]), wuq_ref[...])
    qra, qrb = rot(qm[:, 512:640], qm[:, 640:768], cos_m, sin_m)
    for i in range(4):
        qt_ref[0, 512 + LANES * i:512 + LANES * (i + 1), :] = (qm[:, LANES * i:LANES * (i + 1)] * MLA_SCALE).T.astype(BF16)
    qt_ref[0, 1024:1152, :] = (qra * MLA_SCALE).T.astype(BF16)
    qt_ref[0, 1152:1280, :] = (qrb * MLA_SCALE).T.astype(BF16)

    kvm = _dot(rms(z[:, EV_CKV:EV_CKV + MLA_KV_RANK], kvg_ref[...]), wukv_ref[...])
    k_rope = (blk(EV_KRA) * cos_m + blk(EV_KRB) * sin_ms).astype(BF16)
    for g in range(4):
        km_ref[0, g, :, 0:LANES] = kvm[:, LANES * g:LANES * (g + 1)].astype(BF16)
        km_ref[0, g, :, LANES:2 * LANES] = k_rope
        vm_t = kvm[:, 512 + LANES * g:512 + LANES * (g + 1)].T.astype(BF16)
        for half in range(2):
            r0 = DA_HEADS * VA_ROWS + VM_ROWS * (2 * g + half)
            vt_ref[0, r0:r0 + MLA_V, :] = vm_t[MLA_V * half:MLA_V * (half + 1), :]
            vt_ref[0, r0 + MLA_V:r0 + VM_ROWS, :] = jnp.ones((ONES_ROWS, x.shape[0]), BF16)

    g_ref[0] = _silu(z[:, EV_GATE:EV_GATE + d]).astype(BF16)


def _proj_even(x, mod, rope, w, b, qg, kvg, wuq, wukv, tm):
    bsz, n, d = x.shape
    const = lambda shape: pl.BlockSpec(shape, lambda b_, i: (0,) * len(shape))
    return pl.pallas_call(
        _proj_even_kernel,
        grid=(bsz, n // tm),
        in_specs=[pl.BlockSpec((1, tm, d), lambda b_, i: (b_, i, 0)),
                  _mod_spec(mod),
                  pl.BlockSpec((tm, 5 * LANES), lambda b_, i: (i, 0)),
                  const((d, EV_COLS)), const((1, EV_COLS)), const((1, MLA_Q_RANK)), const((1, MLA_KV_RANK)),
                  const((MLA_Q_RANK, 768)), const((MLA_KV_RANK, 1024))],
        out_specs=[pl.BlockSpec((1, QT_ROWS, tm), lambda b_, i: (b_, 0, i)),
                   pl.BlockSpec((1, 2, tm, 256), lambda b_, i: (b_, 0, i, 0)),
                   pl.BlockSpec((1, 4, tm, 256), lambda b_, i: (b_, 0, i, 0)),
                   pl.BlockSpec((1, VT_ROWS, tm), lambda b_, i: (b_, 0, i)),
                   pl.BlockSpec((1, tm, d), lambda b_, i: (b_, i, 0))],
        out_shape=[jax.ShapeDtypeStruct((bsz, QT_ROWS, n), BF16),
                   jax.ShapeDtypeStruct((bsz, 2, n, 256), BF16),
                   jax.ShapeDtypeStruct((bsz, 4, n, 256), BF16),
                   jax.ShapeDtypeStruct((bsz, VT_ROWS, n), BF16),
                   jax.ShapeDtypeStruct((bsz, n, d), BF16)],
        compiler_params=_cparams(("parallel", "parallel")),
        name="proj_even",
    )(x, mod, rope, w, b, qg, kvg, wuq, wukv)


def _attn_even_kernel(*refs, tq, nkv, sub_tk, has_ctx, lam_init):
    if has_ctx:
        (lam_ref, subg_ref, qt_ref, ka_ref, km_ref, vt_ref, kac_ref, kmc_ref, vtc_ref,
         y_ref, qw, acc_a, acc_m, m_s) = refs
    else:
        (lam_ref, subg_ref, qt_ref, ka_ref, km_ref, vt_ref,
         y_ref, qw, acc_a, acc_m, m_s) = refs
    j = pl.program_id(2)

    def process(ka, km, vt, n_sub, sub):
        def scores(t, u):
            rows = slice(sub * t, sub * (t + 1))
            k = ka[0, u // DA_HEADS, rows, :] if u < 2 * DA_HEADS else km[0, (u - 2 * DA_HEADS) // 2, rows, :]
            return _dot(k, qw[u])

        def softmax_step(u, s):
            m_old = m_s[u]
            m_new = jnp.maximum(m_old, jnp.max(s, axis=0, keepdims=True))
            alpha = jnp.exp2(m_old - m_new)
            p = jnp.exp2(s - m_new)
            m_s[u] = m_new
            return alpha, p.astype(BF16)

        def pv(t, u, alpha, p):
            cols = slice(sub * t, sub * (t + 1))
            if u < 2 * DA_HEADS:
                hd = u % DA_HEADS
                acc_a[u] = alpha * acc_a[u] + _dot(vt[0, VA_ROWS * hd:VA_ROWS * (hd + 1), cols], p)
            else:
                hd = u - 2 * DA_HEADS
                v = vt[0, DA_HEADS * VA_ROWS + VM_ROWS * hd:DA_HEADS * VA_ROWS + VM_ROWS * (hd + 1), cols]
                acc_m[hd] = alpha * acc_m[hd] + _dot(v, p)

        pairs = [(t, u) for t in range(n_sub) for u in range(N_UNITS)]
        s_next = scores(*pairs[0])
        pending = None
        for i, (t, u) in enumerate(pairs):
            s_cur = s_next
            if i + 1 < len(pairs):
                s_next = scores(*pairs[i + 1])
            if pending is not None:
                pv(*pending)
            alpha, p = softmax_step(u, s_cur)
            pending = (t, u, alpha, p)
        pv(*pending)

    @pl.when(j == 0)
    def _init():
        row = lax.broadcasted_iota(jnp.int32, (256, tq), 0)
        head_of_row = (row >> 5) & (DA_HEADS - 1)
        for mp in range(2):
            q = qt_ref[0, 256 * mp:256 * (mp + 1), :]
            for hd in range(DA_HEADS):
                qw[mp * DA_HEADS + hd] = jnp.where(head_of_row == hd, q, jnp.zeros_like(q))
        for hd in range(MLA_HEADS):
            u = 2 * DA_HEADS + hd
            qw[u] = jnp.zeros((256, tq), BF16)
            qw[u, 64 * (hd % 2):64 * (hd % 2) + 64, :] = qt_ref[0, 512 + 64 * hd:512 + 64 * hd + 64, :]
            qw[u, 128:144, :] = qt_ref[0, 1024 + 16 * hd:1024 + 16 * hd + 16, :]
            qw[u, 144:160, :] = qt_ref[0, 1152 + 16 * hd:1152 + 16 * hd + 16, :]
        acc_a[...] = jnp.zeros(acc_a.shape, F32)
        acc_m[...] = jnp.zeros(acc_m.shape, F32)
        m_s[...] = jnp.full(m_s.shape, MASK_VALUE, F32)
        if has_ctx:
            process(kac_ref, kmc_ref, vtc_ref, 1, kac_ref.shape[2])

    process(ka_ref, km_ref, vt_ref, ka_ref.shape[2] // sub_tk, sub_tk)

    @pl.when(j == nkv - 1)
    def _finish():
        lm = lam_ref[...]
        lam = (jnp.exp(jnp.sum(lm[0:1] * lm[1:2], axis=-1, keepdims=True))
               - jnp.exp(jnp.sum(lm[2:3] * lm[3:4], axis=-1, keepdims=True)) + lam_init)
        for hd in range(DA_HEADS):
            a1, a2 = acc_a[hd], acc_a[DA_HEADS + hd]
            o = a1[0:DA_V] / a1[DA_V:DA_V + 1] - lam * (a2[0:DA_V] / a2[DA_V:DA_V + 1])
            o = o * lax.rsqrt(jnp.mean(o * o, axis=0, keepdims=True) + RMS_EPS) * subg_ref[...] * (1.0 - lam_init)
            y_ref[0, :, DA_V * hd:DA_V * (hd + 1)] = o.T.astype(BF16)
        for g in range(MLA_HEADS // 2):
            b1, b2 = acc_m[2 * g], acc_m[2 * g + 1]
            o = jnp.concatenate([b1[0:MLA_V] / b1[MLA_V:MLA_V + 1], b2[0:MLA_V] / b2[MLA_V:MLA_V + 1]], axis=0)
            y_ref[0, :, 512 + LANES * g:512 + LANES * (g + 1)] = o.T.astype(BF16)


def _attn_even(lam_p, subg, qt, ka, km, vt, ctx_kv, lam_init, tq, tk):
    bsz, _, n = qt.shape
    nk = ka.shape[2]
    nq, nkv = n // tq, nk // tk
    has_ctx = ctx_kv is not None
    in_specs = [pl.BlockSpec((4, DA_QK), lambda b, i, j: (0, 0)),
                pl.BlockSpec((DA_V, 1), lambda b, i, j: (0, 0)),
                pl.BlockSpec((1, QT_ROWS, tq), lambda b, i, j: (b, 0, i)),
                pl.BlockSpec((1, 2, tk, 256), lambda b, i, j: (b, 0, j, 0)),
                pl.BlockSpec((1, 4, tk, 256), lambda b, i, j: (b, 0, j, 0)),
                pl.BlockSpec((1, VT_ROWS, tk), lambda b, i, j: (b, 0, j))]
    args = [lam_p, subg, qt, ka, km, vt]
    if has_ctx:
        tc = ctx_kv[0].shape[2]
        in_specs += [pl.BlockSpec((1, 2, tc, 256), lambda b, i, j: (b, 0, 0, 0)),
                     pl.BlockSpec((1, 4, tc, 256), lambda b, i, j: (b, 0, 0, 0)),
                     pl.BlockSpec((1, VT_ROWS, tc), lambda b, i, j: (b, 0, 0))]
        args += list(ctx_kv)
    return pl.pallas_call(
        functools.partial(_attn_even_kernel, tq=tq, nkv=nkv, sub_tk=min(tk, 512), has_ctx=has_ctx, lam_init=lam_init),
        grid=(bsz, nq, nkv),
        in_specs=in_specs,
        out_specs=pl.BlockSpec((1, tq, D_MODEL), lambda b, i, j: (b, i, 0)),
        out_shape=jax.ShapeDtypeStruct((bsz, n, D_MODEL), BF16),
        scratch_shapes=[pltpu.VMEM((N_UNITS, 256, tq), BF16),
                        pltpu.VMEM((2 * DA_HEADS, VA_ROWS, tq), F32),
                        pltpu.VMEM((MLA_HEADS, VM_ROWS, tq), F32),
                        pltpu.VMEM((N_UNITS, 1, tq), F32)],
        compiler_params=_cparams(("parallel", "parallel", "arbitrary")),
        name="attn_even",
    )(*args)


def _layer_norm_rows(t, g, b):
    mu = jnp.mean(t, axis=-1, keepdims=True)
    tc = t - mu
    var = jnp.mean(tc * tc, axis=-1, keepdims=True)
    return tc * lax.rsqrt(var + LN_EPS) * g + b


def _out_even_kernel(y_ref, g_ref, x_ref, mod_ref, w_ref, lng_ref, lnb_ref, o_ref):
    d = D_MODEL
    y = (y_ref[0].astype(F32) * g_ref[0].astype(F32)).astype(BF16)
    out = _dot(y, w_ref[...])
    t = DEEPNORM_ALPHA * x_ref[0] + mod_ref[0][:, 2 * d:3 * d] * out
    o_ref[0] = _layer_norm_rows(t, lng_ref[...], lnb_ref[...])


def _out_odd_kernel(hf_ref, hb_ref, so_ref, na_ref, g_ref, x_ref, mod_ref, w_ref, ng_ref, lng_ref, lnb_ref, o_ref):
    d = D_MODEL
    hsum = hf_ref[0].astype(F32) + hb_ref[0].astype(F32)
    parts = []
    for hd in range(ML_HEADS):
        t = hsum[:, ML_DH * hd:ML_DH * (hd + 1)]
        mu = jnp.mean(t, axis=-1, keepdims=True)
        tc = t - mu
        var = jnp.mean(tc * tc, axis=-1, keepdims=True)
        parts.append(tc * lax.rsqrt(var + LN_EPS))
    y_ml = jnp.concatenate(parts, axis=-1) * ng_ref[...] * so_ref[0].astype(F32)
    g = g_ref[0].astype(F32)
    y = jnp.concatenate([y_ml * g[:, 0:ML_WIDTH], na_ref[0].astype(F32) * g[:, ML_WIDTH:]], axis=-1).astype(BF16)
    out = _dot(y, w_ref[...])
    t = DEEPNORM_ALPHA * x_ref[0] + mod_ref[0][:, 2 * d:3 * d] * out
    o_ref[0] = _layer_norm_rows(t, lng_ref[...], lnb_ref[...])


def _out_proj(kernel_fn, name, acts, x, mod, w, extra, ln_g, ln_b, tm):
    bsz, n, d = x.shape
    tok = lambda a: pl.BlockSpec((1, tm, a.shape[2]), lambda b_, i: (b_, i, 0))
    const = lambda a: pl.BlockSpec(a.shape, lambda b_, i: (0,) * a.ndim)
    ins = list(acts) + [x, mod, w] + list(extra) + [ln_g, ln_b]
    specs = [tok(a) for a in acts] + [tok(x), _mod_spec(mod), const(w)] + [const(e) for e in extra] + [const(ln_g), const(ln_b)]
    return pl.pallas_call(
        kernel_fn,
        grid=(bsz, n // tm),
        in_specs=specs,
        out_specs=pl.BlockSpec((1, tm, d), lambda b_, i: (b_, i, 0)),
        out_shape=jax.ShapeDtypeStruct((bsz, n, d), F32),
        compiler_params=_cparams(("parallel", "parallel")),
        name=name,
    )(*ins)


def _proj_odd_kernel(x_ref, xp_ref, xn_ref, mod_ref, w_ref, b_ref, cw_ref, cb_ref,
                     q_ref, kt_ref, v_ref, so_ref, gt_ref, qn_ref, kn_ref, vn_ref, g_ref, buf, *, tm, nt):
    d = D_MODEL
    i = pl.program_id(1)
    mod = mod_ref[0]
    x = jnp.concatenate([xp_ref[0], x_ref[0], xn_ref[0]], axis=0)
    h = (x * (1.0 + mod[:, d:2 * d]) + mod[:, 0:d]).astype(BF16)
    z_all = _dot(h, w_ref[...]) + b_ref[...]
    z = z_all[8:8 + tm]
    v_ref[0] = z[:, OD_V:OD_V + ML_WIDTH].astype(BF16)
    so_ref[0] = jax.nn.sigmoid(z[:, OD_O:OD_O + ML_WIDTH]).astype(BF16)
    gt_ref[0] = z[:, OD_GATES:OD_GATES + LANES]
    kn_ref[0] = z[:, OD_KN:OD_KN + NA_WIDTH].astype(BF16)
    for g in range(NA_HEADS // 2):
        qn_ref[0, LANES * g:LANES * (g + 1), :] = (z[:, OD_QN + LANES * g:OD_QN + LANES * (g + 1)] * NA_SCALE).T.astype(BF16)
        vn_ref[0, VN_ROWS * g:VN_ROWS * g + LANES, :] = z[:, OD_VN + LANES * g:OD_VN + LANES * (g + 1)].T.astype(BF16)
        vn_ref[0, VN_ROWS * g + LANES:VN_ROWS * (g + 1), :] = jnp.ones((ONES_ROWS, tm), BF16)
    g_ref[0] = _silu(z[:, OD_GATE:OD_GATE + d]).astype(BF16)

    c = 2 * ML_WIDTH
    buf[0:8, :] = jnp.where(i > 0, z_all[0:8, 0:c], 0.0)
    buf[8:8 + tm, :] = z[:, 0:c]
    buf[8 + tm:16 + tm, :] = jnp.where(i < nt - 1, z_all[8 + tm:16 + tm, 0:c], 0.0)
    cw = cw_ref[...]
    acc = cb_ref[...] + buf[6:6 + tm, :] * cw[0:1]
    for tap in range(1, ML_CONV):
        acc = acc + buf[6 + tap:6 + tap + tm, :] * cw[tap:tap + 1]
    y = _silu(acc)
    q_ref[0] = y[:, 0:ML_WIDTH].astype(BF16)
    for hd in range(ML_HEADS):
        k = y[:, ML_WIDTH + ML_DH * hd:ML_WIDTH + ML_DH * (hd + 1)] * (ML_DH ** -0.5)
        kt_ref[0, ML_DH * hd:ML_DH * (hd + 1), :] = k.T.astype(BF16)


def _proj_odd(x, mod, w, b, cw, cb, tm):
    bsz, n, d = x.shape
    nt = n // tm
    r = tm // 8
    last8 = n // 8 - 1
    const = lambda shape: pl.BlockSpec(shape, lambda b_, i: (0,) * len(shape))
    tok = lambda wd, dt: (pl.BlockSpec((1, tm, wd), lambda b_, i: (b_, i, 0)), jax.ShapeDtypeStruct((bsz, n, wd), dt))
    feat = lambda rws, dt: (pl.BlockSpec((1, rws, tm), lambda b_, i: (b_, 0, i)), jax.ShapeDtypeStruct((bsz, rws, n), dt))
    outs = [tok(ML_WIDTH, BF16), feat(ML_WIDTH, BF16), tok(ML_WIDTH, BF16), tok(ML_WIDTH, BF16), tok(LANES, F32),
            feat(NA_WIDTH, BF16), tok(NA_WIDTH, BF16), feat(VN_ROWS * NA_HEADS // 2, BF16), tok(d, BF16)]
    return pl.pallas_call(
        functools.partial(_proj_odd_kernel, tm=tm, nt=nt),
        grid=(bsz, nt),
        in_specs=[pl.BlockSpec((1, tm, d), lambda b_, i: (b_, i, 0)),
                  pl.BlockSpec((1, 8, d), lambda b_, i: (b_, jnp.maximum(i * r - 1, 0), 0)),
                  pl.BlockSpec((1, 8, d), lambda b_, i: (b_, jnp.minimum((i + 1) * r, last8), 0)),
                  _mod_spec(mod), const((d, OD_COLS)), const((1, OD_COLS)),
                  const((ML_CONV, 2 * ML_WIDTH)), const((1, 2 * ML_WIDTH))],
        out_specs=[o[0] for o in outs],
        out_shape=[o[1] for o in outs],
        scratch_shapes=[pltpu.VMEM((tm + 16, 2 * ML_WIDTH), F32)],
        compiler_params=_cparams(("parallel", "parallel")),
        name="proj_odd",
    )(x, x, x, mod, w, b, cw, cb)


def _scan_rows(x, reverse, op, fill):
    n = x.shape[0]
    row = lax.broadcasted_iota(jnp.int32, x.shape, 0)
    sh = 1
    while sh < n:
        if reverse:
            x = op(x, jnp.where(row + sh < n, pltpu.roll(x, n - sh, 0), fill))
        else:
            x = op(x, jnp.where(row >= sh, pltpu.roll(x, sh, 0), fill))
        sh *= 2
    return x


def _mlstm_kernel(qf_ref, ktf_ref, vf_ref, gf_ref, qb_ref, ktb_ref, vb_ref, gb_ref, fb_ref, c0_ref, m0_ref,
                  hf_ref, hb_ref, c1_ref, m1_ref, c_s, m_s, *, nc):
    L, d = ML_CHUNK, ML_DH
    step = pl.program_id(1)

    @pl.when(step == 0)
    def _load_state():
        c_s[...] = c0_ref[0]
        m_s[...] = m0_ref[0]

    lane = lax.broadcasted_iota(jnp.int32, (L, LANES), 1)
    is_forget = (lane & ML_HEADS) != 0
    ti = lax.broadcasted_iota(jnp.int32, (L, L), 0)
    si = lax.broadcasted_iota(jnp.int32, (L, L), 1)
    ones = jnp.ones((L, d), BF16)
    dirs = ((False, qf_ref, ktf_ref, vf_ref, gf_ref, hf_ref), (True, qb_ref, ktb_ref, vb_ref, gb_ref, hb_ref))
    qk_all = [_dot(q_ref[0, :, d * hd:d * (hd + 1)], kt_ref[0, d * hd:d * (hd + 1), :])
              for (_, q_ref, kt_ref, _, _, _) in dirs for hd in range(ML_HEADS)]
    gate = []
    for di, (reverse, q_ref, kt_ref, v_ref, g_ref, h_ref) in enumerate(dirs):
        gates = g_ref[0]
        g2 = jnp.where(is_forget, _log_sigmoid(gates + fb_ref[...]), gates)
        a_cum = _scan_rows(jnp.where(is_forget, g2, 0.0), reverse, jnp.add, 0.0)
        b_t = pltpu.roll(g2, ML_HEADS, 1) - a_cum
        last = 0 if reverse else L - 1
        a_tot = a_cum[last:last + 1, :]
        m_old = m_s[di]
        mm = jnp.maximum(m_old, jnp.max(b_t, axis=0, keepdims=True))
        w_s_t = jnp.exp(b_t - mm).T
        decay = jnp.exp(m_old - mm)
        m_s[di] = a_tot + mm
        big_m = jnp.maximum(m_old, _scan_rows(b_t, reverse, jnp.maximum, MASK_VALUE))
        m_t = a_cum + big_m
        causal = (ti <= si) if reverse else (ti >= si)
        gate.append((m_old, w_s_t, decay, big_m, m_t, b_t.T, causal))

    results = []
    for di, (reverse, q_ref, kt_ref, v_ref, g_ref, h_ref) in enumerate(dirs):
        m_old, w_s_t, decay, big_m, m_t, b_rows, causal = gate[di]
        for hd in range(ML_HEADS):
            f = 2 * ML_HEADS * di + ML_HEADS + hd
            ch = ML_HEADS * di + hd
            q = q_ref[0, :, d * hd:d * (hd + 1)]
            kt = kt_ref[0, d * hd:d * (hd + 1), :]
            vext = jnp.concatenate([v_ref[0, :, d * hd:d * (hd + 1)], ones], axis=-1)
            c_old = c_s[ch]
            big_m_b = jnp.broadcast_to(big_m[:, f:f + 1], (L, L))
            m_old_b = jnp.broadcast_to(m_old[:, f:f + 1], (L, d))
            w_ts = jnp.exp(jnp.where(causal, b_rows[f:f + 1, :] - big_m_b, MASK_VALUE)) * qk_all[ch]
            w_c = jnp.exp(m_old_b - big_m_b)
            lhs = jnp.concatenate([w_ts.astype(BF16), (q.astype(F32) * w_c).astype(BF16)], axis=-1)
            rhs = jnp.concatenate([vext, c_old.astype(BF16)], axis=0)
            out = _dot(lhs, rhs)
            kw_t = (kt.astype(F32) * w_s_t[f:f + 1, :]).astype(BF16)
            m_t_b = jnp.broadcast_to(m_t[:, f:f + 1], (L, d))
            results.append((out, m_t_b, decay[:, f:f + 1] * c_old + _dot(kw_t, vext)))

    for di, (reverse, q_ref, kt_ref, v_ref, g_ref, h_ref) in enumerate(dirs):
        for hd in range(ML_HEADS):
            ch = ML_HEADS * di + hd
            out, m_t_b, c_new = results[ch]
            h = out[:, 0:d] / jnp.maximum(jnp.abs(out[:, d:]), jnp.exp(-m_t_b))
            h_ref[0, :, d * hd:d * (hd + 1)] = h.astype(BF16)
            c_s[ch] = c_new

    @pl.when(step == nc - 1)
    def _store_state():
        c1_ref[0] = c_s[...]
        m1_ref[0] = m_s[...]


def _mlstm(q, kt, v, gates, fb_lane, state):
    bsz, n, _ = q.shape
    nc = n // ML_CHUNK
    L, w = ML_CHUNK, ML_WIDTH
    fwd = lambda b_, s: (b_, s, 0)
    bwd = lambda b_, s: (b_, nc - 1 - s, 0)
    fwd_t = lambda b_, s: (b_, 0, s)
    bwd_t = lambda b_, s: (b_, 0, nc - 1 - s)
    st = lambda b_, s: (b_, 0, 0, 0)
    c0, m0 = state
    c_shape, m_shape = (2 * ML_HEADS, ML_DH, 2 * ML_DH), (2, 1, LANES)
    tok = lambda width, im: pl.BlockSpec((1, L, width), im)
    out = pl.pallas_call(
        functools.partial(_mlstm_kernel, nc=nc),
        grid=(bsz, nc),
        in_specs=[tok(w, fwd), pl.BlockSpec((1, w, L), fwd_t), tok(w, fwd), tok(LANES, fwd),
                  tok(w, bwd), pl.BlockSpec((1, w, L), bwd_t), tok(w, bwd), tok(LANES, bwd),
                  pl.BlockSpec((1, LANES), lambda b_, s: (0, 0)),
                  pl.BlockSpec((1,) + c_shape, st), pl.BlockSpec((1,) + m_shape, st)],
        out_specs=[tok(w, fwd), tok(w, bwd), pl.BlockSpec((1,) + c_shape, st), pl.BlockSpec((1,) + m_shape, st)],
        out_shape=[jax.ShapeDtypeStruct((bsz, n, w), BF16), jax.ShapeDtypeStruct((bsz, n, w), BF16),
                   jax.ShapeDtypeStruct((bsz,) + c_shape, F32), jax.ShapeDtypeStruct((bsz,) + m_shape, F32)],
        scratch_shapes=[pltpu.VMEM(c_shape, F32), pltpu.VMEM(m_shape, F32)],
        compiler_params=_cparams(("parallel", "arbitrary")),
        name="mlstm",
    )(q, kt, v, gates, q, kt, v, gates, fb_lane, c0, m0)
    return out[0], out[1], (out[2], out[3])


def _na_kernel(*refs, n_blocks):
    if n_blocks:
        q_ref, kc_ref, vc_ref, kp_ref, k0_ref, k1_ref, vp_ref, v0_ref, v1_ref, bias_ref, o_ref = refs
        k_lat, v_lat = (kp_ref, k0_ref, k1_ref), (vp_ref, v0_ref, v1_ref)
    else:
        q_ref, kc_ref, vc_ref, o_ref = refs
        k_lat, v_lat = (), ()
    qb = q_ref.shape[2]
    row = lax.broadcasted_iota(jnp.int32, (LANES, qb), 0)

    def scores(hd):
        g, half = hd // 2, hd % 2
        q2 = q_ref[0, LANES * g:LANES * (g + 1), :]
        q = jnp.where((row >> 6) == half, q2, jnp.zeros_like(q2))
        sl = slice(LANES * g, LANES * (g + 1))
        parts = [_dot(kr[0, :, sl], q) for kr in k_lat]
        if n_blocks:
            bias = bias_ref[0, hd]
            parts = [p + bias[qb * t:qb * (t + 1), :] for t, p in enumerate(parts)]
        parts.append(_dot(kc_ref[0, :, sl], q))
        return parts

    def softmax(parts):
        mx = functools.reduce(jnp.maximum, [jnp.max(p, axis=0, keepdims=True) for p in parts])
        return [jnp.exp2(p - mx).astype(BF16) for p in parts]

    def weighted_values(hd, probs):
        g, half = hd // 2, hd % 2
        rows = slice(VN_ROWS * g, VN_ROWS * (g + 1))
        acc = None
        for p, vr in zip(probs, v_lat + (vc_ref,)):
            t = _dot(vr[0, rows, :], p)
            acc = t if acc is None else acc + t
        return acc[NA_DH * half:NA_DH * (half + 1), :] / acc[LANES:LANES + 1, :]

    all_scores = [scores(hd) for hd in range(NA_HEADS)]
    all_probs = [softmax(parts) for parts in all_scores]
    outs = [weighted_values(hd, probs) for hd, probs in enumerate(all_probs)]
    for g in range(NA_HEADS // 2):
        o_ref[0, :, LANES * g:LANES * (g + 1)] = jnp.concatenate(outs[2 * g:2 * g + 2], axis=0).T.astype(BF16)


def _na_attn(qnt, kn, vnt, kc, vct, bias, qb):
    bsz, w, n = qnt.shape
    t = kc.shape[1]
    n_blocks = 0 if bias is None else 3
    nb = n // qb
    vr = vct.shape[1]
    tok = lambda off: pl.BlockSpec((1, qb, w), lambda b_, i: (b_, jnp.clip(i + off, 0, nb - 1), 0))
    feat = lambda r, off: pl.BlockSpec((1, r, qb), lambda b_, i: (b_, 0, jnp.clip(i + off, 0, nb - 1)))
    ins = [qnt, kc, vct]
    specs = [feat(w, 0), pl.BlockSpec((1, t, w), lambda b_, i: (b_, 0, 0)), pl.BlockSpec((1, vr, t), lambda b_, i: (b_, 0, 0))]
    if n_blocks:
        ins += [kn, kn, kn, vnt, vnt, vnt, bias]
        block_class = lambda b_, i: (jnp.where(i == 0, 0, jnp.where(i == nb - 1, 2, 1)), 0, 0, 0)
        specs += [tok(-1), tok(0), tok(1), feat(vr, -1), feat(vr, 0), feat(vr, 1),
                  pl.BlockSpec((1,) + bias.shape[1:], block_class)]
    return pl.pallas_call(
        functools.partial(_na_kernel, n_blocks=n_blocks),
        grid=(bsz, nb),
        in_specs=specs,
        out_specs=pl.BlockSpec((1, qb, w), lambda b_, i: (b_, i, 0)),
        out_shape=jax.ShapeDtypeStruct((bsz, n, w), BF16),
        compiler_params=_cparams(("parallel", "parallel")),
        name="na_attn" if n_blocks else "ctx_attn",
    )(*ins)


def _even_cols(w):
    lead = w.shape[:-1]
    half, hr = DA_QK // 2, MLA_ROPE // 2
    qk = w[..., :1024].reshape(lead + (4, DA_HEADS, 2, half))
    qk = jnp.swapaxes(qk, -3, -2).reshape(lead + (1024,))
    kr = w[..., 1920:1952]
    pad = jnp.zeros(lead + (LANES - MLA_ROPE,), w.dtype)
    kr_swapped = jnp.concatenate([kr[..., hr:], kr[..., :hr]], axis=-1)
    return jnp.concatenate([qk, w[..., 1024:1920], kr, pad, kr_swapped, pad, w[..., 1952:]], axis=-1)


def _uq_cols(w):
    hr = MLA_ROPE // 2
    w3 = w.reshape(w.shape[0], MLA_HEADS, MLA_NOPE + MLA_ROPE)
    parts = [w3[..., :MLA_NOPE], w3[..., MLA_NOPE:MLA_NOPE + hr], w3[..., MLA_NOPE + hr:]]
    return jnp.concatenate([p.reshape(w.shape[0], -1) for p in parts], axis=-1)


def _ukv_cols(w):
    w3 = w.reshape(w.shape[0], MLA_HEADS, MLA_NOPE + MLA_V)
    return jnp.concatenate([w3[..., :MLA_NOPE].reshape(w.shape[0], -1), w3[..., MLA_NOPE:].reshape(w.shape[0], -1)], axis=-1)


def _odd_cols(w):
    pad = jnp.zeros(w.shape[:-1] + (LANES - 4 * ML_HEADS,), w.dtype)
    return jnp.concatenate([w[..., :2064], pad, w[..., 2064:]], axis=-1)


def _rope_tables(n_lat, n_ctx):
    t = jnp.arange(n_lat)
    row = (t // GRID_W).astype(F32)
    col = (t % GRID_W).astype(F32)

    def cs(dim):
        nf = dim // 4
        freqs = ROPE_BASE ** (-jnp.arange(nf, dtype=F32) / nf)
        ang = jnp.concatenate([row[:, None] * freqs, col[:, None] * freqs], axis=-1)
        return jnp.cos(ang), jnp.sin(ang)

    ca, sa = cs(DA_QK)
    cm, sm = cs(MLA_ROPE)
    lat = jnp.concatenate([jnp.tile(ca, (1, 4)), jnp.tile(sa, (1, 4)), jnp.tile(cm, (1, 8)), jnp.tile(sm, (1, 8)),
                           jnp.tile(jnp.concatenate([-sm, sm], axis=-1), (1, 4))], axis=-1)
    one, zero = jnp.ones((n_ctx, LANES), F32), jnp.zeros((n_ctx, LANES), F32)
    return lat, jnp.concatenate([one, zero, one, zero, zero], axis=-1)


def _na_bias_tiles(rpb, qb, rows):
    heads, n_dr, n_dc = rpb.shape
    rows_per = qb // GRID_W
    assert rows >= 3 * rows_per and NA_ROWS == 2 * rows_per
    kc, qc = np.arange(GRID_W)[:, None], np.arange(GRID_W)[None, :]
    c0 = np.clip(qc - NA_COLS // 2, 0, GRID_W - NA_COLS)
    col_ok = (kc >= c0) & (kc < c0 + NA_COLS)
    onehot = ((kc - qc + NA_COLS - 1)[None] == np.arange(n_dc)[:, None, None]) & col_ok[None]
    tiles = jnp.einsum('hrj,jp->hrp', rpb * LOG2E, jnp.asarray(onehot.reshape(n_dc, -1), F32),
                       precision=lax.Precision.HIGHEST).reshape(heads, n_dr, GRID_W, GRID_W)
    tiles = jnp.where(jnp.asarray(col_ok)[None, None], tiles, MASK_VALUE)
    kr = (np.arange(3 * rows_per) - rows_per)[:, None]
    qr = np.arange(rows_per)[None, :]
    dr = kr - qr
    window = [(kr >= 0) & (kr < NA_ROWS) & (qr >= 0),
              (dr >= -(NA_ROWS // 2)) & (dr < NA_ROWS // 2),
              (kr >= rows_per - NA_ROWS) & (kr < rows_per) & (qr >= 0)]
    masked = jnp.full((heads, GRID_W, GRID_W), MASK_VALUE, F32)
    classes = []
    for ok in window:
        key_rows = [jnp.concatenate([tiles[:, dr[a, b] + NA_ROWS - 1] if ok[a, b] else masked
                                     for b in range(rows_per)], axis=-1) for a in range(3 * rows_per)]
        classes.append(jnp.concatenate(key_rows, axis=1))
    return jnp.stack(classes)


def _tile_plan(n_lat, n_ctx):
    return {
        "tok_lat": 512,
        "tok_odd": 512,
        "tok_ctx": min(256, n_ctx),
        "attn_q": 1024 if n_lat % 1024 == 0 else 512,
        "attn_k": 1024 if n_lat % 1024 == 0 else 512,
        "na_q": 4 * GRID_W,
    }


def kernel(x, c, ctx, c_ctx, ada_w, ada_b, ln_g, ln_b, ev_w_in, ev_b_in, da_lambda, da_subln_g, mla_q_norm_g, mla_kv_norm_g, mla_w_uq, mla_w_ukv, ev_w_out, od_w_in, od_b_in, ml_conv_w, ml_conv_b, ml_f_bias, ml_norm_g, na_rpb, od_w_out):
    bsz, n_lat, d = x.shape
    n_ctx = ctx.shape[1]
    rows = n_lat // GRID_W
    assert d == D_MODEL and bsz + 1 <= 8 and n_lat % 512 == 0 and n_ctx % 256 == 0 and rows >= NA_ROWS

    cond8 = jnp.zeros((8, d), F32).at[:bsz].set(c).at[bsz].set(c_ctx)
    mod_all = _ada_mod(cond8, ada_w, ada_b)
    rope_lat, rope_ctx = _rope_tables(n_lat, n_ctx)

    tp = _tile_plan(n_lat, n_ctx)
    tm_lat, tm_ctx, tq, tk, na_qb = tp["tok_lat"], tp["tok_ctx"], tp["attn_q"], tp["attn_k"], tp["na_q"]
    xl, xc = x, ctx
    for l in range(DEPTH):
        update_ctx = l < DEPTH - 1
        i = l // 2
        mod_l = mod_all[l, :bsz].reshape(bsz, 1, 3 * d)
        mod_c = mod_all[l, bsz].reshape(1, 1, 3 * d)
        lng, lnb = ln_g[l].reshape(1, d), ln_b[l].reshape(1, d)
        if l % 2 == 0:
            lam_init = 0.8 - 0.6 * math.exp(-0.3 * l)
            w = _even_cols(ev_w_in[i].astype(BF16))
            b = _even_cols(ev_b_in[i]).reshape(1, EV_COLS)
            wuq = _uq_cols(mla_w_uq[i].astype(BF16))
            wukv = _ukv_cols(mla_w_ukv[i].astype(BF16))
            qg = mla_q_norm_g[i].reshape(1, MLA_Q_RANK)
            kvg = mla_kv_norm_g[i].reshape(1, MLA_KV_RANK)
            subg = da_subln_g[i].reshape(DA_V, 1)
            wout = ev_w_out[i].astype(BF16)
            qt_l, ka_l, km_l, vt_l, g_l = _proj_even(xl, mod_l, rope_lat, w, b, qg, kvg, wuq, wukv, tm_lat)
            qt_c, ka_c, km_c, vt_c, g_c = _proj_even(xc, mod_c, rope_ctx, w, b, qg, kvg, wuq, wukv, tm_ctx)
            y_l = _attn_even(da_lambda[i], subg, qt_l, ka_l, km_l, vt_l, (ka_c, km_c, vt_c), lam_init, tq, tk)
            xl_new = _out_proj(_out_even_kernel, "out_even", (y_l, g_l), xl, mod_l, wout, (), lng, lnb, tm_lat)
            if update_ctx:
                y_c = _attn_even(da_lambda[i], subg, qt_c, ka_c, km_c, vt_c, None, lam_init, n_ctx, n_ctx)
                xc = _out_proj(_out_even_kernel, "out_even", (y_c, g_c), xc, mod_c, wout, (), lng, lnb, tm_ctx)
            xl = xl_new
        else:
            w = _odd_cols(od_w_in[i].astype(BF16))
            b = _odd_cols(od_b_in[i]).reshape(1, OD_COLS)
            wout = od_w_out[i].astype(BF16)
            cw, cb = ml_conv_w[i], ml_conv_b[i].reshape(1, 2 * ML_WIDTH)
            fb = ml_f_bias[i].astype(F32)
            fb_lane = jnp.zeros((1, LANES), F32).at[0, 4:8].set(fb[0]).at[0, 12:16].set(fb[1])
            ng = ml_norm_g[i].reshape(1, ML_WIDTH)
            q_l, kt_l, v_l, so_l, gt_l, qn_l, kn_l, vn_l, g_l = _proj_odd(xl, mod_l, w, b, cw, cb, tp["tok_odd"])
            q_c, kt_c, v_c, so_c, gt_c, qn_c, kn_c, vn_c, g_c = _proj_odd(xc, mod_c, w, b, cw, cb, tm_ctx)
            zero = (jnp.zeros((bsz, 2 * ML_HEADS, ML_DH, 2 * ML_DH), F32), jnp.zeros((bsz, 2, 1, LANES), F32))
            hf_c, hb_c, st_c = _mlstm(q_c, kt_c, v_c, gt_c, fb_lane, zero)
            hf_l, hb_l, _ = _mlstm(q_l, kt_l, v_l, gt_l, fb_lane, st_c)
            bias = _na_bias_tiles(na_rpb[i].astype(F32), na_qb, rows)
            na_l = _na_attn(qn_l, kn_l, vn_l, kn_c, vn_c, bias, na_qb)
            xl_new = _out_proj(_out_odd_kernel, "out_odd", (hf_l, hb_l, so_l, na_l, g_l), xl, mod_l, wout, (ng,), lng, lnb, tm_lat)
            if update_ctx:
                na_c = _na_attn(qn_c, None, None, kn_c, vn_c, None, n_ctx)
                xc = _out_proj(_out_odd_kernel, "out_odd", (hf_c, hb_c, so_c, na_c, g_c), xc, mod_c, wout, (ng,), lng, lnb, tm_ctx)
            xl = xl_new
    return xl
```
